```python
import jax, jax.numpy as jnp
from jax import lax
import numpy as np

D_MODEL = 1024
BATCH = 4
SEQ = 8192
DEPTH = 2
DEC_BATCH = 32
DEC_SEQ = 1
PAST_LEN = 16384
PAGE_SIZE = 128

BRANCH_WIDTH = D_MODEL // 2
CONV_CH = BRANCH_WIDTH
CONV_WIDTH = 31
HEAD_DIM = 64
ATT_HEADS = BRANCH_WIDTH // HEAD_DIM
ATT_WIDTH = ATT_HEADS * HEAD_DIM
MOBA_BLOCK = 256
MOBA_TOPK = 3
MOBA_Q_CHUNK = 64
ROPE_THETA = 10000.0
SSM_HEAD_DIM = 64
SSM_INNER = BRANCH_WIDTH
SSM_HEADS = SSM_INNER // SSM_HEAD_DIM
SSM_GROUPS = 2
SSM_STATE = 128
SSM_CONV = 4
SSD_CHUNK = 128
XBC_DIM = SSM_INNER + 2 * SSM_GROUPS * SSM_STATE
N_BRANCHES = 3
FFN_HIDDEN = -(-8 * D_MODEL // (3 * 256)) * 256
IN_COLS = 2 * CONV_CH + 3 * ATT_WIDTH + SSM_INNER + XBC_DIM + SSM_HEADS + N_BRANCHES * D_MODEL
NORM_EPS = 1e-6

kernel_name = 'hybrid_conformer_moba_ssd_decode_step'


def _in_offsets():
    sizes = [2 * CONV_CH, ATT_WIDTH, ATT_WIDTH, ATT_WIDTH, SSM_INNER, XBC_DIM, SSM_HEADS, N_BRANCHES * D_MODEL]
    return [int(o) for o in np.cumsum(sizes)[:-1]]


def rmsnorm(x, g):
    xf = x.astype(jnp.float32)
    y = xf * lax.rsqrt(jnp.mean(xf * xf, axis=-1, keepdims=True) + NORM_EPS)
    return (y * g.astype(jnp.float32)).astype(x.dtype)


def layernorm(x, g, b):
    xf = x.astype(jnp.float32)
    mu = jnp.mean(xf, axis=-1, keepdims=True)
    var = jnp.mean(jnp.square(xf - mu), axis=-1, keepdims=True)
    y = (xf - mu) * lax.rsqrt(var + NORM_EPS)
    return (y * g.astype(jnp.float32) + b.astype(jnp.float32)).astype(x.dtype)


def rope(x, pos):
    half = HEAD_DIM // 2
    inv = jnp.power(ROPE_THETA, -jnp.arange(half, dtype=jnp.float32) * (2.0 / HEAD_DIM))
    ang = pos.astype(jnp.float32)[:, None] * inv[None, :]
    cos = jnp.cos(ang)[None, :, None, :]
    sin = jnp.sin(ang)[None, :, None, :]
    xf = x.astype(jnp.float32)
    x1, x2 = xf[..., :half], xf[..., half:]
    return jnp.concatenate([x1 * cos - x2 * sin, x2 * cos + x1 * sin], axis=-1).astype(x.dtype)


def depthwise_causal_conv(u_ext, w, b):
    y = lax.conv_general_dilated(u_ext, w[:, None, :].astype(u_ext.dtype), window_strides=(1,), padding='VALID',
                                 dimension_numbers=('NWC', 'WIO', 'NWC'), feature_group_count=u_ext.shape[-1])
    return y + b


def conv_branch(glu_in, buf, w, b, ln_g, ln_b):
    a, g = jnp.split(glu_in, 2, axis=-1)
    u = a * jax.nn.sigmoid(g)
    u_ext = jnp.concatenate([buf.astype(u.dtype), u], axis=1)
    y = jax.nn.silu(layernorm(depthwise_causal_conv(u_ext, w, b), ln_g, ln_b))
    return y, u_ext[:, -(CONV_WIDTH - 1):]


def ssd_scan(x, dt, a, bm, cm, h0):
    bsz, L, H, P = x.shape
    q = min(SSD_CHUNK, L)
    pad = (-L) % q
    if pad:
        padw = lambda t: jnp.pad(t, [(0, 0), (0, pad)] + [(0, 0)] * (t.ndim - 2))
        x, dt, bm, cm = padw(x), padw(dt), padw(bm), padw(cm)
    nc = (L + pad) // q

    def chunks(t):
        return jnp.moveaxis(t.reshape((bsz, nc, q) + t.shape[2:]), 1, 0)

    tril = jnp.tril(jnp.ones((q, q), dtype=bool))

    def step(h, inp):
        xc, dtc, bc, cc = inp
        xf, bf, cf = xc.astype(jnp.float32), bc.astype(jnp.float32), cc.astype(jnp.float32)
        cum = jnp.cumsum(dtc * a, axis=1)
        seg = cum[:, :, None, :] - cum[:, None, :, :]
        lmat = jnp.exp(jnp.where(tril[None, :, :, None], seg, -jnp.inf))
        xdt = xf * dtc[..., None]
        y = jnp.einsum('bijh,bjhp->bihp', jnp.einsum('bihn,bjhn->bijh', cf, bf) * lmat, xdt)
        y = y + jnp.einsum('bihn,bhpn->bihp', cf, h) * jnp.exp(cum)[..., None]
        decay = jnp.exp(cum[:, -1:, :] - cum)
        h = h * jnp.exp(cum[:, -1, :])[:, :, None, None] + jnp.einsum('bjhn,bjhp->bhpn', bf * decay[..., None], xdt)
        return h, y

    h, ys = lax.scan(step, h0.astype(jnp.float32), (chunks(x), chunks(dt), chunks(bm), chunks(cm)))
    y = jnp.moveaxis(ys, 0, 1).reshape(bsz, nc * q, H, P)[:, :L]
    return y.astype(x.dtype), h.astype(x.dtype)


def ssm_branch(z, xbc, dt_raw, conv_buf, h0, conv_w, conv_b, dt_bias, a_log, d_skip, norm_g):
    bsz, L, _ = z.shape
    xbc_ext = jnp.concatenate([conv_buf.astype(xbc.dtype), xbc], axis=1)
    xbc_c = jax.nn.silu(depthwise_causal_conv(xbc_ext, conv_w, conv_b))
    xs, bm, cm = jnp.split(xbc_c, [SSM_INNER, SSM_INNER + SSM_GROUPS * SSM_STATE], axis=-1)
    xs = xs.reshape(bsz, L, SSM_HEADS, SSM_HEAD_DIM)
    rep = SSM_HEADS // SSM_GROUPS
    bm = jnp.repeat(bm.reshape(bsz, L, SSM_GROUPS, SSM_STATE), rep, axis=2)
    cm = jnp.repeat(cm.reshape(bsz, L, SSM_GROUPS, SSM_STATE), rep, axis=2)
    dt = jax.nn.softplus(dt_raw.astype(jnp.float32) + dt_bias.astype(jnp.float32))
    a = -jnp.exp(a_log.astype(jnp.float32))
    y, h = ssd_scan(xs, dt, a, bm, cm, h0)
    y = (y + d_skip[:, None] * xs).reshape(bsz, L, SSM_INNER)
    y = rmsnorm(y * jax.nn.silu(z), norm_g)
    return y, xbc_ext[:, -(SSM_CONV - 1):], h


def moba_blocks(k, v):
    bsz, L, H, Dh = k.shape
    nb = -(-L // MOBA_BLOCK)
    pad = nb * MOBA_BLOCK - L
    k = jnp.pad(k, ((0, 0), (0, pad), (0, 0), (0, 0)))
    v = jnp.pad(v, ((0, 0), (0, pad), (0, 0), (0, 0)))
    kb = k.reshape(bsz, nb, MOBA_BLOCK, H, Dh).transpose(0, 3, 1, 2, 4)
    vb = v.reshape(bsz, nb, MOBA_BLOCK, H, Dh).transpose(0, 3, 1, 2, 4)
    kmean = jnp.mean(kb.astype(jnp.float32), axis=3)
    return kb, vb, kmean


def moba_query(q, q_pos, kb, vb, kmean):
    bsz, sq, H, Dh = q.shape
    nb = kb.shape[2]
    tk = min(MOBA_TOPK, nb)
    qh = jnp.swapaxes(q, 1, 2)
    own = q_pos // MOBA_BLOCK
    gate = jnp.einsum('bhqd,bhnd->bhqn', qh.astype(jnp.float32), kmean)
    gate = jnp.where(jnp.arange(nb)[None, :] < own[:, None], gate, -jnp.inf)
    _, top = lax.top_k(gate, tk)
    sel = jnp.concatenate([top.astype(jnp.int32),
                           jnp.broadcast_to(own.astype(jnp.int32)[None, None, :, None], (bsz, H, sq, 1))], axis=-1)
    bi = jnp.arange(bsz)[:, None, None, None]
    hi = jnp.arange(H)[None, :, None, None]
    ks = kb[bi, hi, sel]
    vs = vb[bi, hi, sel]
    logits = jnp.einsum('bhqd,bhqjkd->bhqjk', qh, ks, preferred_element_type=jnp.float32) * (HEAD_DIM ** -0.5)
    slot = jnp.arange(tk + 1)
    keypos = own[:, None] * MOBA_BLOCK + jnp.arange(MOBA_BLOCK)[None, :]
    own_ok = keypos <= q_pos[:, None]
    past_ok = slot[None, :] < own[:, None]
    valid = jnp.where((slot == tk)[None, :, None], own_ok[:, None, :], past_ok[:, :, None])
    logits = jnp.where(valid, logits, -jnp.inf)
    p = jax.nn.softmax(logits.reshape(bsz, H, sq, -1), axis=-1).reshape(logits.shape)
    out = jnp.einsum('bhqjk,bhqjkd->bhqd', p.astype(vs.dtype), vs)
    return jnp.swapaxes(out, 1, 2)


def moba_prompt(q, k, v):
    bsz, L, H, Dh = q.shape
    kb, vb, kmean = moba_blocks(k, v)
    nq = L // MOBA_Q_CHUNK
    qc = jnp.moveaxis(q.reshape(bsz, nq, MOBA_Q_CHUNK, H, Dh), 1, 0)
    pc = jnp.arange(L, dtype=jnp.int32).reshape(nq, MOBA_Q_CHUNK)
    out = lax.map(lambda args: moba_query(args[0], args[1], kb, vb, kmean), (qc, pc))
    return jnp.moveaxis(out, 0, 1).reshape(bsz, L, H, Dh)


def trunk_layer(x, pos, k_past, v_past, conv_buf, ssm_conv_buf, ssm_h, lw):
    bsz, L, _ = x.shape
    hn = rmsnorm(x, lw['norm1_g'])
    glu_in, q, k, v, z, xbc, dt_raw, gate_logits = jnp.split(hn @ lw['w_in'], _in_offsets(), axis=-1)
    o_conv, new_conv_buf = conv_branch(glu_in, conv_buf, lw['conv_w'], lw['conv_b'], lw['conv_ln_g'], lw['conv_ln_b'])
    q = rope(rmsnorm(q.reshape(bsz, L, ATT_HEADS, HEAD_DIM), lw['q_norm_g']), pos)
    k = rope(rmsnorm(k.reshape(bsz, L, ATT_HEADS, HEAD_DIM), lw['k_norm_g']), pos)
    v = v.reshape(bsz, L, ATT_HEADS, HEAD_DIM)
    if k_past is None:
        att = moba_prompt(q, k, v)
    else:
        kb, vb, kmean = moba_blocks(jnp.concatenate([k_past.astype(k.dtype), k], axis=1),
                                    jnp.concatenate([v_past.astype(v.dtype), v], axis=1))
        att = moba_query(q, pos, kb, vb, kmean)
    o_att = att.reshape(bsz, L, ATT_WIDTH)
    o_ssm, new_ssm_conv_buf, new_h = ssm_branch(z, xbc, dt_raw, ssm_conv_buf, ssm_h, lw['ssm_conv_w'], lw['ssm_conv_b'],
                                                lw['dt_bias'], lw['a_log'], lw['d_skip'], lw['ssm_norm_g'])
    g_conv, g_att, g_ssm = jnp.split(jax.nn.sigmoid(gate_logits), N_BRANCHES, axis=-1)
    merged = (g_conv * (o_conv @ lw['w_branch_conv']) + g_att * (o_att @ lw['w_branch_att'])
              + g_ssm * (o_ssm @ lw['w_branch_ssm']))
    x = x + merged @ lw['w_out']
    a, g = jnp.split(rmsnorm(x, lw['norm2_g']) @ lw['w_ffn_in'], 2, axis=-1)
    x = x + (jax.nn.silu(a) * g) @ lw['w_ffn_out']
    return x, k, v, new_conv_buf, new_ssm_conv_buf, new_h


def setup_inputs(seed: int = 0) -> dict:
    key = jax.random.key(seed)
    ks = jax.random.split(key, 40)
    f32 = jnp.float32
    n_pages = PAST_LEN // PAGE_SIZE
    n_used = DEC_BATCH * n_pages
    n_pool = n_used + max(1, n_used // 4)
    nrm = lambda k, shape, scale: jax.random.normal(k, shape, f32) * scale
    page_table = jax.random.permutation(ks[0], n_pool)[:n_used].reshape(DEC_BATCH, n_pages).astype(jnp.int32)
    dt0 = jnp.exp(jax.random.uniform(ks[1], (DEPTH, SSM_HEADS), f32, jnp.log(1e-3), jnp.log(1e-1)))
    return {
        'x_prompt': nrm(ks[2], (BATCH, SEQ, D_MODEL), 1.0),
        'x_sample': nrm(ks[3], (DEC_BATCH, DEC_SEQ, D_MODEL), 1.0),
        'cache_k': nrm(ks[4], (DEPTH, n_pool, PAGE_SIZE, ATT_HEADS, HEAD_DIM), 1.0),
        'cache_v': nrm(ks[5], (DEPTH, n_pool, PAGE_SIZE, ATT_HEADS, HEAD_DIM), 1.0),
        'state_conv': nrm(ks[6], (DEPTH, DEC_BATCH, CONV_WIDTH - 1, CONV_CH), 0.5),
        'state_ssm_conv': nrm(ks[7], (DEPTH, DEC_BATCH, SSM_CONV - 1, XBC_DIM), 0.5),
        'state_ssm': nrm(ks[8], (DEPTH, DEC_BATCH, SSM_HEADS, SSM_HEAD_DIM, SSM_STATE), 0.3),
        'page_table': page_table,
        'norm1_g': 1.0 + nrm(ks[9], (DEPTH, D_MODEL), 0.05),
        'w_in': nrm(ks[10], (DEPTH, D_MODEL, IN_COLS), D_MODEL ** -0.5),
        'conv_w': nrm(ks[11], (DEPTH, CONV_WIDTH, CONV_CH), CONV_WIDTH ** -0.5),
        'conv_b': nrm(ks[12], (DEPTH, CONV_CH), 0.02),
        'conv_ln_g': 1.0 + nrm(ks[13], (DEPTH, CONV_CH), 0.05),
        'conv_ln_b': nrm(ks[14], (DEPTH, CONV_CH), 0.02),
        'q_norm_g': 1.0 + nrm(ks[15], (DEPTH, HEAD_DIM), 0.05),
        'k_norm_g': 1.0 + nrm(ks[16], (DEPTH, HEAD_DIM), 0.05),
        'ssm_conv_w': nrm(ks[17], (DEPTH, SSM_CONV, XBC_DIM), SSM_CONV ** -0.5),
        'ssm_conv_b': nrm(ks[18], (DEPTH, XBC_DIM), 0.02),
        'dt_bias': dt0 + jnp.log(-jnp.expm1(-dt0)),
        'a_log': jnp.log(jax.random.uniform(ks[19], (DEPTH, SSM_HEADS), f32, 1.0, 16.0)),
        'd_skip': 1.0 + nrm(ks[20], (DEPTH, SSM_HEADS), 0.05),
        'ssm_norm_g': 1.0 + nrm(ks[21], (DEPTH, SSM_INNER), 0.05),
        'w_branch_conv': nrm(ks[22], (DEPTH, CONV_CH, D_MODEL), CONV_CH ** -0.5),
        'w_branch_att': nrm(ks[23], (DEPTH, ATT_WIDTH, D_MODEL), ATT_WIDTH ** -0.5),
        'w_branch_ssm': nrm(ks[24], (DEPTH, SSM_INNER, D_MODEL), SSM_INNER ** -0.5),
        'w_out': nrm(ks[25], (DEPTH, D_MODEL, D_MODEL), D_MODEL ** -0.5),
        'norm2_g': 1.0 + nrm(ks[26], (DEPTH, D_MODEL), 0.05),
        'w_ffn_in': nrm(ks[27], (DEPTH, D_MODEL, 2 * FFN_HIDDEN), D_MODEL ** -0.5),
        'w_ffn_out': nrm(ks[28], (DEPTH, FFN_HIDDEN, D_MODEL), FFN_HIDDEN ** -0.5),
    }


def reference(x_prompt, x_sample, cache_k, cache_v, state_conv, state_ssm_conv, state_ssm, page_table,
              norm1_g, w_in, conv_w, conv_b, conv_ln_g, conv_ln_b, q_norm_g, k_norm_g, ssm_conv_w, ssm_conv_b,
              dt_bias, a_log, d_skip, ssm_norm_g, w_branch_conv, w_branch_att, w_branch_ssm, w_out, norm2_g,
              w_ffn_in, w_ffn_out):
    bp, lp, _ = x_prompt.shape
    bs, ls, _ = x_sample.shape
    past_len = page_table.shape[1] * cache_k.shape[2]
    pos_p = jnp.arange(lp, dtype=jnp.int32)
    pos_s = past_len + jnp.arange(ls, dtype=jnp.int32)
    hp, hs = x_prompt, x_sample
    kp_l, vp_l, cp_l, scp_l, sp_l = [], [], [], [], []
    ks_l, vs_l, cs_l, scs_l, ss_l = [], [], [], [], []
    for l in range(DEPTH):
        lw = {'norm1_g': norm1_g[l], 'w_in': w_in[l], 'conv_w': conv_w[l], 'conv_b': conv_b[l],
              'conv_ln_g': conv_ln_g[l], 'conv_ln_b': conv_ln_b[l], 'q_norm_g': q_norm_g[l], 'k_norm_g': k_norm_g[l],
              'ssm_conv_w': ssm_conv_w[l], 'ssm_conv_b': ssm_conv_b[l], 'dt_bias': dt_bias[l], 'a_log': a_log[l],
              'd_skip': d_skip[l], 'ssm_norm_g': ssm_norm_g[l], 'w_branch_conv': w_branch_conv[l],
              'w_branch_att': w_branch_att[l], 'w_branch_ssm': w_branch_ssm[l], 'w_out': w_out[l],
              'norm2_g': norm2_g[l], 'w_ffn_in': w_ffn_in[l], 'w_ffn_out': w_ffn_out[l]}
        zc = jnp.zeros((bp, CONV_WIDTH - 1, CONV_CH), hp.dtype)
        zsc = jnp.zeros((bp, SSM_CONV - 1, XBC_DIM), hp.dtype)
        zh = jnp.zeros((bp, SSM_HEADS, SSM_HEAD_DIM, SSM_STATE), hp.dtype)
        hp, kp, vp, cp, scp, sp = trunk_layer(hp, pos_p, None, None, zc, zsc, zh, lw)
        k_past = cache_k[l][page_table].reshape(bs, past_len, ATT_HEADS, HEAD_DIM)
        v_past = cache_v[l][page_table].reshape(bs, past_len, ATT_HEADS, HEAD_DIM)
        hs, k_s, v_s, c_s, sc_s, s_s = trunk_layer(hs, pos_s, k_past, v_past, state_conv[l], state_ssm_conv[l],
                                                  state_ssm[l], lw)
        kp_l.append(kp); vp_l.append(vp); cp_l.append(cp); scp_l.append(scp); sp_l.append(sp)
        ks_l.append(k_s); vs_l.append(v_s); cs_l.append(c_s); scs_l.append(sc_s); ss_l.append(s_s)
    return (hp, hs,
            jnp.stack(kp_l), jnp.stack(vp_l), jnp.stack(cp_l), jnp.stack(scp_l), jnp.stack(sp_l),
            jnp.stack(ks_l), jnp.stack(vs_l), jnp.stack(cs_l), jnp.stack(scs_l), jnp.stack(ss_l))
```

```python
import functools

import numpy as np
import jax
import jax.numpy as jnp
from jax import lax
from jax.experimental import pallas as pl
from jax.experimental.pallas import tpu as pltpu

F32 = jnp.float32
BF16 = jnp.bfloat16

CONV_WIDTH = 31
HEAD_DIM = 64
MOBA_BLOCK = 256
MOBA_TOPK = 3
ROPE_THETA = 10000.0
SSM_HEAD_DIM = 64
SSM_GROUPS = 2
SSM_STATE = 128
SSM_CONV = 4
SSD_CHUNK = 128
NORM_EPS = 1e-6

LANES = 128
SUBLANES = 8
VMEM_LIMIT_BYTES = 56 * 1024 * 1024

NEG_BIG = -1e30
HEADS_PER_LANE_TILE = LANES // HEAD_DIM


def _cparams(sem=None):
    return pltpu.CompilerParams(dimension_semantics=sem, vmem_limit_bytes=VMEM_LIMIT_BYTES)


def _resident(shape):
    nd = len(shape)
    return pl.BlockSpec(shape, lambda *_: (0,) * nd, pipeline_mode=pl.Buffered(1))


def _split3(x):
    hi = x.astype(BF16)
    r1 = x - hi.astype(F32)
    mid = r1.astype(BF16)
    lo = (r1 - mid.astype(F32)).astype(BF16)
    return hi, mid, lo


def _dot_nt(a, b):
    return lax.dot_general(a, b, (((1,), (1,)), ((), ())), preferred_element_type=F32)


def _dot_f32_by_exact(a, b_bf16, nt=False):
    f = (lambda p: _dot_nt(p, b_bf16)) if nt else (lambda p: jnp.dot(p, b_bf16, preferred_element_type=F32))
    hi, mid, lo = _split3(a)
    return f(hi) + f(mid) + f(lo)


def _sigmoid(x):
    return 1.0 / (1.0 + jnp.exp(-x))


def _silu(x):
    return x * _sigmoid(x)


def _softplus(x):
    return jnp.maximum(x, 0.0) + jnp.log(1.0 + jnp.exp(-jnp.abs(x)))


def _rmsnorm_rows(x, gain):
    return (x * lax.rsqrt(jnp.mean(x * x, axis=-1, keepdims=True) + NORM_EPS)) * gain


def _proj_kernel(x_ref, g1_ref, wm_ref, wdt_ref, wg_ref, cos_ref, sin_ref, qg_ref, kg_ref, pm_ref,
                 u_ref, q_ref, k_ref, v_ref, z_ref, xbc_ref, dt_ref, gate_ref, *block_refs,
                 widths, blocks_per_tile):
    cw, aw, sw, xw = widths
    x = x_ref[...]
    hn = _rmsnorm_rows(x, g1_ref[...]).astype(BF16)

    def proj(lo, width):
        return jnp.dot(hn, wm_ref[:, lo:lo + width], preferred_element_type=F32)

    off = 0
    a = proj(off, cw)
    g = proj(off + cw, cw)
    u_ref[...] = a * _sigmoid(g)
    off += 2 * cw

    lane = lax.broadcasted_iota(jnp.int32, (x.shape[0], aw), 1)
    first_half = (lane % HEAD_DIM) < (HEAD_DIM // 2)
    cos = cos_ref[...]
    sin = sin_ref[...]

    def qk_norm_rope(t, gain_ref):
        ss = jnp.dot((t * t).astype(BF16), pm_ref[...], preferred_element_type=F32)
        tn = (t * lax.rsqrt(ss + NORM_EPS)) * gain_ref[...]
        up = pltpu.roll(tn, aw - HEAD_DIM // 2, axis=1)
        dn = pltpu.roll(tn, HEAD_DIM // 2, axis=1)
        return tn * cos + jnp.where(first_half, up, dn) * sin

    q = qk_norm_rope(proj(off, aw), qg_ref)
    q_ref[...] = (q * (HEAD_DIM ** -0.5)).astype(q_ref.dtype)
    off += aw
    k = qk_norm_rope(proj(off, aw), kg_ref)
    k_ref[...] = k
    off += aw
    v = proj(off, aw)
    v_ref[...] = v
    off += aw
    z_ref[...] = proj(off, sw)
    off += sw
    xbc_ref[...] = proj(off, xw)
    dt_ref[...] = jnp.dot(hn, wdt_ref[...], preferred_element_type=F32)
    gate_ref[...] = _sigmoid(jnp.dot(hn, wg_ref[...], preferred_element_type=F32)).astype(BF16)
    if blocks_per_tile:
        kb_ref, vb_ref, km_ref = block_refs
        kb_ref[...] = k.astype(BF16)
        vb_ref[...] = v.astype(BF16)
        for c in range(blocks_per_tile):
            km_ref[0, c:c + 1, :] = jnp.mean(k[c * MOBA_BLOCK:(c + 1) * MOBA_BLOCK], axis=0, keepdims=True)


def _rope_tables(pos, n_heads):
    half = HEAD_DIM // 2
    inv = jnp.power(ROPE_THETA, -jnp.arange(half, dtype=F32) * (2.0 / HEAD_DIM))
    ang = pos.astype(F32)[:, None] * inv[None, :]
    cos = jnp.cos(ang)
    sin = jnp.sin(ang)
    cos_h = jnp.concatenate([cos, cos], axis=-1)
    sin_h = jnp.concatenate([-sin, sin], axis=-1)
    return jnp.tile(cos_h, (1, n_heads)), jnp.tile(sin_h, (1, n_heads))


def _head_mean_matrix(width):
    idx = np.arange(width) // HEAD_DIM
    return jnp.asarray((idx[:, None] == idx[None, :]).astype(np.float32) / HEAD_DIM, dtype=BF16)


def _project(x2d, pw, cos, sin, dims, tile_rows, seq_tiles, block_stats):
    d_model, cw, aw, sw, xw, _ = dims
    n = x2d.shape[0]
    tm = tile_rows
    assert n % tm == 0
    bpt = tm // MOBA_BLOCK if block_stats else 0
    assert not block_stats or tm % MOBA_BLOCK == 0
    n_gate = pw['wg'].shape[1]
    row = lambda w: pl.BlockSpec((tm, w), lambda i: (i, 0))
    tab = pl.BlockSpec((tm, aw), lambda i: (i % seq_tiles, 0))
    in_specs = [row(d_model), _resident((1, d_model)), _resident(pw['wm'].shape), _resident(pw['wdt'].shape),
                _resident(pw['wg'].shape), tab, tab, _resident((1, aw)), _resident((1, aw)), _resident((aw, aw))]
    sds = jax.ShapeDtypeStruct
    out_shape = [sds((n, cw), F32), sds((n, aw), BF16 if block_stats else F32), sds((n, aw), F32),
                 sds((n, aw), F32), sds((n, sw), F32), sds((n, xw), F32), sds((n, LANES), F32),
                 sds((n, n_gate), BF16)]
    out_specs = [row(cw), row(aw), row(aw), row(aw), row(sw), row(xw), row(LANES), row(n_gate)]
    if block_stats:
        out_shape += [sds((n, aw), BF16), sds((n, aw), BF16), sds((n // tm, bpt, aw), F32)]
        out_specs += [row(aw), row(aw), pl.BlockSpec((1, bpt, aw), lambda i: (i, 0, 0))]
    return pl.pallas_call(
        functools.partial(_proj_kernel, widths=(cw, aw, sw, xw), blocks_per_tile=bpt),
        grid=(n // tm,), in_specs=in_specs, out_specs=out_specs, out_shape=out_shape,
        compiler_params=_cparams(("parallel",)), name="in_proj",
    )(x2d, pw['g1'], pw['wm'], pw['wdt'], pw['wg'], cos, sin, pw['qg'], pw['kg'], pw['pm'])


CONV_HALO = 32


def _conv_seq_kernel(u_ref, w_ref, b_ref, lg_ref, lb_ref, o_ref, ext_ref, *, tile):
    @pl.when(pl.program_id(1) == 0)
    def _():
        ext_ref[0:CONV_HALO, :] = jnp.zeros((CONV_HALO, ext_ref.shape[1]), F32)

    ext_ref[CONV_HALO:CONV_HALO + tile, :] = u_ref[0]
    base = CONV_HALO - (CONV_WIDTH - 1)
    acc = ext_ref[base:base + tile, :] * w_ref[0:1, :]
    for kk in range(1, CONV_WIDTH):
        acc = acc + ext_ref[base + kk:base + kk + tile, :] * w_ref[kk:kk + 1, :]
    y = acc + b_ref[...]
    mu = jnp.mean(y, axis=-1, keepdims=True)
    yc = y - mu
    var = jnp.mean(yc * yc, axis=-1, keepdims=True)
    yn = (yc * lax.rsqrt(var + NORM_EPS)) * lg_ref[...] + lb_ref[...]
    o_ref[0] = _silu(yn).astype(o_ref.dtype)
    ext_ref[0:CONV_HALO, :] = ext_ref[tile:tile + CONV_HALO, :]


def _conv_branch_seq(u, lw, tile):
    bsz, L, c = u.shape
    assert L % tile == 0 and tile >= CONV_HALO
    vec = lambda a: a.reshape(1, c)
    return pl.pallas_call(
        functools.partial(_conv_seq_kernel, tile=tile),
        grid=(bsz, L // tile),
        in_specs=[pl.BlockSpec((1, tile, c), lambda b, l: (b, l, 0)), _resident((CONV_WIDTH, c)),
                  _resident((1, c)), _resident((1, c)), _resident((1, c))],
        out_specs=pl.BlockSpec((1, tile, c), lambda b, l: (b, l, 0)),
        out_shape=jax.ShapeDtypeStruct((bsz, L, c), BF16),
        scratch_shapes=[pltpu.VMEM((CONV_HALO + tile, c), F32)],
        compiler_params=_cparams(("parallel", "arbitrary")), name="conv_branch",
    )(u, lw['conv_w'], vec(lw['conv_b']), vec(lw['conv_ln_g']), vec(lw['conv_ln_b']))


def _moba_seq_kernel(q_ref, k_ref, v_ref, km_ref, o_ref, *, n_blocks):
    i = pl.program_id(2)
    blk = MOBA_BLOCK
    q_pair = q_ref[0]
    lane = lax.broadcasted_iota(jnp.int32, (blk, LANES), 1)
    row = lax.broadcasted_iota(jnp.int32, (blk, blk), 0)
    col = lax.broadcasted_iota(jnp.int32, (blk, blk), 1)
    causal = col <= row
    blk_id = lax.broadcasted_iota(jnp.int32, (blk, n_blocks), 1)
    km_hi, km_mid, km_lo = _split3(km_ref[0])

    k_own = k_ref[0, pl.ds(pl.multiple_of(i * blk, blk), blk), :]
    v_own = v_ref[0, pl.ds(pl.multiple_of(i * blk, blk), blk), :]

    qs, sels, ms, ls, accs = [], [], [], [], []
    for h in range(HEADS_PER_LANE_TILE):
        in_head = (lane // HEAD_DIM) == h
        qh = jnp.where(in_head, q_pair, jnp.zeros_like(q_pair))
        qs.append(qh)
        gate = _dot_nt(qh, km_hi) + _dot_nt(qh, km_mid) + _dot_nt(qh, km_lo)
        gate = jnp.where(blk_id < i, gate, -jnp.inf)
        picks = []
        for _ in range(MOBA_TOPK):
            best = jnp.max(gate, axis=-1, keepdims=True)
            idx = jnp.min(jnp.where(gate == best, blk_id, n_blocks), axis=-1, keepdims=True)
            picks.append(idx)
            gate = jnp.where(blk_id == idx, -jnp.inf, gate)
        sels.append(picks)
        s = _dot_nt(qh, k_own)
        s = jnp.where(causal, s, NEG_BIG)
        m = jnp.max(s, axis=-1, keepdims=True)
        p = jnp.exp(s - m)
        ms.append(m)
        ls.append(jnp.sum(p, axis=-1, keepdims=True))
        accs.append(jnp.dot(p.astype(BF16), v_own, preferred_element_type=F32))

    def body(j, carry):
        ms, ls, accs = carry
        kj = k_ref[0, pl.ds(pl.multiple_of(j * blk, blk), blk), :]
        vj = v_ref[0, pl.ds(pl.multiple_of(j * blk, blk), blk), :]
        new_m, new_l, new_acc = [], [], []
        for h in range(HEADS_PER_LANE_TILE):
            p0, p1, p2 = sels[h]
            chosen = (p0 == j) | (p1 == j) | (p2 == j)
            s = jnp.where(chosen, _dot_nt(qs[h], kj), NEG_BIG)
            m = jnp.maximum(ms[h], jnp.max(s, axis=-1, keepdims=True))
            alpha = jnp.exp(ms[h] - m)
            p = jnp.exp(s - m)
            new_m.append(m)
            new_l.append(ls[h] * alpha + jnp.sum(p, axis=-1, keepdims=True))
            new_acc.append(accs[h] * alpha + jnp.dot(p.astype(BF16), vj, preferred_element_type=F32))
        return tuple(new_m), tuple(new_l), tuple(new_acc)

    ms, ls, accs = lax.fori_loop(0, i, body, (tuple(ms), tuple(ls), tuple(accs)))
    out = jnp.where(lane < HEAD_DIM, accs[0] / ls[0], accs[1] / ls[1])
    o_ref[0] = out.astype(o_ref.dtype)


def _moba_seq(q, k, v, kmean):
    bsz, L, w = q.shape
    nb = L // MOBA_BLOCK
    assert L % MOBA_BLOCK == 0 and w % LANES == 0
    kv_spec = pl.BlockSpec((1, L, LANES), lambda b, hp, i: (b, 0, hp))
    return pl.pallas_call(
        functools.partial(_moba_seq_kernel, n_blocks=nb),
        grid=(bsz, w // LANES, nb),
        in_specs=[pl.BlockSpec((1, MOBA_BLOCK, LANES), lambda b, hp, i: (b, i, hp)), kv_spec, kv_spec,
                  pl.BlockSpec((1, nb, LANES), lambda b, hp, i: (b, 0, hp))],
        out_specs=pl.BlockSpec((1, MOBA_BLOCK, LANES), lambda b, hp, i: (b, i, hp)),
        out_shape=jax.ShapeDtypeStruct((bsz, L, w), BF16),
        compiler_params=_cparams(("parallel", "parallel", "arbitrary")), name="moba_prompt",
    )(q, k, v, kmean)


SSM_HALO = SUBLANES


def _ssd_seq_kernel(xbc_ref, dt_ref, z_ref, cw_ref, cb_ref, dtb_ref, a_ref, dsk_ref, ng_ref, tri_ref,
                    o_ref, h_ref, ext_ref, y_ref, *, tile, inner, n_pairs):
    q = SSD_CHUNK
    gs = SSM_GROUPS * SSM_STATE

    @pl.when(pl.program_id(1) == 0)
    def _():
        ext_ref[0:SSM_HALO, :] = jnp.zeros((SSM_HALO, ext_ref.shape[1]), F32)
        h_ref[...] = jnp.zeros_like(h_ref)

    ext_ref[SSM_HALO:SSM_HALO + tile, :] = xbc_ref[0]
    base = SSM_HALO - (SSM_CONV - 1)
    acc = ext_ref[base:base + tile, :] * cw_ref[0:1, :]
    for kk in range(1, SSM_CONV):
        acc = acc + ext_ref[base + kk:base + kk + tile, :] * cw_ref[kk:kk + 1, :]
    xc = _silu(acc + cb_ref[...])
    ext_ref[0:SSM_HALO, :] = ext_ref[tile:tile + SSM_HALO, :]

    dt = _softplus(dt_ref[0] + dtb_ref[...])
    da = dt * a_ref[...]
    tri = tri_ref[...]
    rows = lax.broadcasted_iota(jnp.int32, (q, q), 0)
    cols = lax.broadcasted_iota(jnp.int32, (q, q), 1)
    lower = cols <= rows
    lane = lax.broadcasted_iota(jnp.int32, (q, LANES), 1)
    left = lane < SSM_HEAD_DIM
    top = lax.broadcasted_iota(jnp.int32, (LANES, SSM_STATE), 0) < SSM_HEAD_DIM

    for c in range(tile // q):
        r0 = c * q
        hi, mid, lo = _split3(da[r0:r0 + q])
        cum = (jnp.dot(tri, hi, preferred_element_type=F32) + jnp.dot(tri, mid, preferred_element_type=F32)
               + jnp.dot(tri, lo, preferred_element_type=F32))
        cum_t = cum.T
        e_cum = jnp.exp(cum)
        e_rest = jnp.exp(cum[q - 1:q, :] - cum)
        e_last_t = jnp.exp(cum_t[:, q - 1:q])
        dt_c = dt[r0:r0 + q]
        bm = xc[r0:r0 + q, inner:inner + gs]
        cm = xc[r0:r0 + q, inner + gs:inner + 2 * gs]
        for p in range(n_pairs):
            g = (p * HEADS_PER_LANE_TILE * SSM_HEAD_DIM) // (inner // SSM_GROUPS)
            b_g = bm[:, g * SSM_STATE:(g + 1) * SSM_STATE].astype(BF16)
            c_g = cm[:, g * SSM_STATE:(g + 1) * SSM_STATE].astype(BF16)
            cb = _dot_nt(c_g, b_g)
            h0, h1 = HEADS_PER_LANE_TILE * p, HEADS_PER_LANE_TILE * p + 1
            xs = xc[r0:r0 + q, p * LANES:(p + 1) * LANES]
            xdt = xs * jnp.where(left, dt_c[:, h0:h0 + 1], dt_c[:, h1:h1 + 1])
            xdt_b = xdt.astype(BF16)
            y_pair = None
            for hh, head in enumerate((h0, h1)):
                seg = cum[:, head:head + 1] - cum_t[head:head + 1, :]
                lmat = jnp.where(lower, jnp.exp(seg), 0.0)
                y_h = jnp.dot((cb * lmat).astype(BF16), xdt_b, preferred_element_type=F32)
                y_pair = y_h if hh == 0 else jnp.where(left, y_pair, y_h)
            h_pair = h_ref[0, p * LANES:(p + 1) * LANES, :]
            inter = _dot_nt(c_g, h_pair.astype(BF16))
            y_pair = y_pair + inter * jnp.where(left, e_cum[:, h0:h0 + 1], e_cum[:, h1:h1 + 1])
            y_ref[r0:r0 + q, p * LANES:(p + 1) * LANES] = y_pair
            xdd = xdt * jnp.where(left, e_rest[:, h0:h0 + 1], e_rest[:, h1:h1 + 1])
            upd = jnp.dot(xdd.T.astype(BF16), b_g, preferred_element_type=F32)
            keep = jnp.where(top, e_last_t[h0:h0 + 1, :], e_last_t[h1:h1 + 1, :])
            h_ref[0, p * LANES:(p + 1) * LANES, :] = h_pair * keep + upd

    y = y_ref[...] + dsk_ref[...] * xc[:, 0:inner]
    gated = y * _silu(z_ref[0])
    o_ref[0] = _rmsnorm_rows(gated, ng_ref[...]).astype(o_ref.dtype)


def _ssd_seq(xbc, dt_raw, z, lw, tile):
    bsz, L, xw = xbc.shape
    inner = z.shape[-1]
    n_heads = inner // SSM_HEAD_DIM
    assert L % tile == 0 and tile % SSD_CHUNK == 0 and inner % LANES == 0
    assert (inner // SSM_GROUPS) % LANES == 0
    pad_h = lambda a: jnp.pad(a, (0, LANES - n_heads)).reshape(1, LANES)
    tri = jnp.asarray(np.tril(np.ones((SSD_CHUNK, SSD_CHUNK), np.float32)), dtype=BF16)
    seq = lambda w: pl.BlockSpec((1, tile, w), lambda b, l: (b, l, 0))
    return pl.pallas_call(
        functools.partial(_ssd_seq_kernel, tile=tile, inner=inner, n_pairs=inner // LANES),
        grid=(bsz, L // tile),
        in_specs=[seq(xw), seq(LANES), seq(inner), _resident((SSM_CONV, xw)), _resident((1, xw)),
                  _resident((1, LANES)), _resident((1, LANES)), _resident((1, inner)), _resident((1, inner)),
                  _resident((SSD_CHUNK, SSD_CHUNK))],
        out_specs=[seq(inner), pl.BlockSpec((1, inner, SSM_STATE), lambda b, l: (b, 0, 0))],
        out_shape=[jax.ShapeDtypeStruct((bsz, L, inner), BF16),
                   jax.ShapeDtypeStruct((bsz, inner, SSM_STATE), F32)],
        scratch_shapes=[pltpu.VMEM((SSM_HALO + tile, xw), F32), pltpu.VMEM((tile, inner), F32)],
        compiler_params=_cparams(("parallel", "arbitrary")), name="ssd_scan",
    )(xbc, dt_raw, z, lw['ssm_conv_w'], lw['ssm_conv_b'].reshape(1, xw), pad_h(lw['dt_bias']),
      pad_h(-jnp.exp(lw['a_log'])), jnp.repeat(lw['d_skip'], SSM_HEAD_DIM).reshape(1, inner),
      lw['ssm_norm_g'].reshape(1, inner), tri)


def _merge_kernel(x_ref, oc_ref, oa_ref, os_ref, gate_ref, wc_ref, wa_ref, ws_ref, wo_ref, o_ref):
    d = x_ref.shape[1]
    gates = gate_ref[...]
    branch = lambda o, w: jnp.dot(o[...], w[...], preferred_element_type=F32)
    merged = (gates[:, 0:d].astype(F32) * branch(oc_ref, wc_ref)
              + gates[:, d:2 * d].astype(F32) * branch(oa_ref, wa_ref)
              + gates[:, 2 * d:3 * d].astype(F32) * branch(os_ref, ws_ref))
    o_ref[...] = x_ref[...] + jnp.dot(merged.astype(BF16), wo_ref[...], preferred_element_type=F32)


def _merge(x2d, o_conv, o_att, o_ssm, gates, mw, tile_rows):
    n, d = x2d.shape
    tm = tile_rows
    assert n % tm == 0
    row = lambda w: pl.BlockSpec((tm, w), lambda i: (i, 0))
    bw = o_conv.shape[1]
    return pl.pallas_call(
        _merge_kernel, grid=(n // tm,),
        in_specs=[row(d), row(bw), row(bw), row(bw), row(3 * d), _resident((bw, d)), _resident((bw, d)),
                  _resident((bw, d)), _resident((d, d))],
        out_specs=row(d), out_shape=jax.ShapeDtypeStruct((n, d), F32),
        compiler_params=_cparams(("parallel",)), name="merge_out_proj",
    )(x2d, o_conv, o_att, o_ssm, gates, mw['wc'], mw['wa'], mw['ws'], mw['wo'])


def _ffn_kernel(x_ref, g2_ref, wa_ref, wg_ref, wo_ref, o_ref, *, chunk):
    x = x_ref[...]
    hn = _rmsnorm_rows(x, g2_ref[...]).astype(BF16)
    acc = x
    for c in range(wa_ref.shape[1] // chunk):
        sl = slice(c * chunk, (c + 1) * chunk)
        a = jnp.dot(hn, wa_ref[:, sl], preferred_element_type=F32)
        g = jnp.dot(hn, wg_ref[:, sl], preferred_element_type=F32)
        acc = acc + jnp.dot((_silu(a) * g).astype(BF16), wo_ref[sl, :], preferred_element_type=F32)
    o_ref[...] = acc


def _ffn_chunk(hidden):
    best = LANES
    for c in range(LANES, hidden + 1, LANES):
        if hidden % c == 0 and c <= 1536:
            best = c
    return best


def _ffn(x2d, fw, tile_rows):
    n, d = x2d.shape
    tm = tile_rows
    hidden = fw['wa'].shape[1]
    assert n % tm == 0 and hidden % LANES == 0
    row = pl.BlockSpec((tm, d), lambda i: (i, 0))
    return pl.pallas_call(
        functools.partial(_ffn_kernel, chunk=_ffn_chunk(hidden)), grid=(n // tm,),
        in_specs=[row, _resident((1, d)), _resident((d, hidden)), _resident((d, hidden)), _resident((hidden, d))],
        out_specs=row, out_shape=jax.ShapeDtypeStruct((n, d), F32),
        compiler_params=_cparams(("parallel",)), name="swiglu_ffn",
    )(x2d, fw['g2'], fw['wa'], fw['wg'], fw['wo'])


def _decode_mix_kernel(u_ref, cst_ref, cw_ref, cb_ref, lg_ref, lb_ref, xbc_ref, sst_ref, sw_ref, sb_ref,
                       dt_ref, dtb_ref, a_ref, ex_ref,
                       oc_ref, xs_ref, bm_ref, cm_ref, xdt_ref, dec_ref, *, inner):
    gs = SSM_GROUPS * SSM_STATE
    acc = u_ref[...] * cw_ref[CONV_WIDTH - 1:CONV_WIDTH, :] + cb_ref[...]
    for kk in range(CONV_WIDTH - 1):
        acc = acc + cst_ref[kk] * cw_ref[kk:kk + 1, :]
    mu = jnp.mean(acc, axis=-1, keepdims=True)
    yc = acc - mu
    var = jnp.mean(yc * yc, axis=-1, keepdims=True)
    oc_ref[...] = _silu((yc * lax.rsqrt(var + NORM_EPS)) * lg_ref[...] + lb_ref[...]).astype(oc_ref.dtype)

    acc = xbc_ref[...] * sw_ref[SSM_CONV - 1:SSM_CONV, :] + sb_ref[...]
    for kk in range(SSM_CONV - 1):
        acc = acc + sst_ref[kk] * sw_ref[kk:kk + 1, :]
    xc = _silu(acc)
    xs = xc[:, 0:inner]
    xs_ref[...] = xs
    bm_ref[...] = xc[:, inner:inner + gs]
    cm_ref[...] = xc[:, inner + gs:inner + 2 * gs]
    dt = _softplus(dt_ref[...] + dtb_ref[...])
    dec = jnp.exp(dt * a_ref[...])
    xdt_ref[...] = xs * _dot_f32_by_exact(dt, ex_ref[...])
    dec_ref[...] = _dot_f32_by_exact(dec, ex_ref[...])


def _decode_mix(u, conv_state, xbc, ssm_conv_state, dt_raw, lw, inner):
    s, cw = u.shape
    xw = xbc.shape[1]
    n_heads = inner // SSM_HEAD_DIM
    gs = SSM_GROUPS * SSM_STATE
    pad_h = lambda a: jnp.pad(a, (0, LANES - n_heads)).reshape(1, LANES)
    expand = np.zeros((LANES, inner), np.float32)
    for h in range(n_heads):
        expand[h, h * SSM_HEAD_DIM:(h + 1) * SSM_HEAD_DIM] = 1.0
    sds = jax.ShapeDtypeStruct
    return pl.pallas_call(
        functools.partial(_decode_mix_kernel, inner=inner),
        out_shape=[sds((s, cw), BF16), sds((s, inner), F32), sds((s, gs), F32), sds((s, gs), F32),
                   sds((s, inner), F32), sds((s, inner), F32)],
        compiler_params=_cparams(), name="decode_conv_ssm_inputs",
    )(u, jnp.swapaxes(conv_state, 0, 1), lw['conv_w'], lw['conv_b'].reshape(1, cw),
      lw['conv_ln_g'].reshape(1, cw), lw['conv_ln_b'].reshape(1, cw),
      xbc, jnp.swapaxes(ssm_conv_state, 0, 1), lw['ssm_conv_w'], lw['ssm_conv_b'].reshape(1, xw),
      dt_raw, pad_h(lw['dt_bias']), pad_h(-jnp.exp(lw['a_log'])), jnp.asarray(expand, dtype=BF16))


def _decode_state_kernel(h_ref, xcol_ref, dcol_ref, xdt_ref, dec_ref, xs_ref, z_ref, bm_ref, cm_ref,
                         dsk_ref, ng_ref, hn_ref, o_ref, *, inner):
    half = inner // SSM_GROUPS
    h = h_ref[0]
    row = lax.broadcasted_iota(jnp.int32, h.shape, 0)
    b0 = bm_ref[0][:, 0:SSM_STATE]
    b1 = bm_ref[0][:, SSM_STATE:2 * SSM_STATE]
    c0 = cm_ref[0][:, 0:SSM_STATE]
    c1 = cm_ref[0][:, SSM_STATE:2 * SSM_STATE]
    hn_ref[0] = h * dcol_ref[0] + xcol_ref[0] * jnp.where(row < half, b0, b1)

    def c_dot_h(c_row, h_rows):
        ch, cm_, _ = _split3(jnp.broadcast_to(c_row, (SUBLANES, SSM_STATE)))
        hh, hm, _ = _split3(h_rows)
        return (_dot_nt(ch, hh) + _dot_nt(ch, hm) + _dot_nt(cm_, hh))[0:1, :]

    ch = jnp.concatenate([c_dot_h(c0, h[0:half]), c_dot_h(c1, h[half:inner])], axis=1)
    lane = lax.broadcasted_iota(jnp.int32, (1, inner), 1)
    cb = jnp.where(lane < half, jnp.sum(c0 * b0, axis=-1, keepdims=True), jnp.sum(c1 * b1, axis=-1, keepdims=True))
    y = ch * dec_ref[0] + cb * xdt_ref[0] + dsk_ref[...] * xs_ref[0]
    gated = y * _silu(z_ref[0])
    o_ref[0] = _rmsnorm_rows(gated, ng_ref[...]).astype(o_ref.dtype)


def _decode_state(h0, xdt, dec, xs, z, bm, cm, lw, inner):
    s = h0.shape[0]
    gs = SSM_GROUPS * SSM_STATE
    per = lambda shape: pl.BlockSpec((1,) + shape, lambda b: (b, 0, 0))
    r3 = lambda a: a.reshape(s, 1, a.shape[-1])
    return pl.pallas_call(
        functools.partial(_decode_state_kernel, inner=inner), grid=(s,),
        in_specs=[per((inner, SSM_STATE)), per((inner, 1)), per((inner, 1)), per((1, inner)), per((1, inner)),
                  per((1, inner)), per((1, inner)), per((1, gs)), per((1, gs)), _resident((1, inner)),
                  _resident((1, inner))],
        out_specs=[per((inner, SSM_STATE)), per((1, inner))],
        out_shape=[jax.ShapeDtypeStruct((s, inner, SSM_STATE), F32), jax.ShapeDtypeStruct((s, 1, inner), BF16)],
        compiler_params=_cparams(("parallel",)), name="decode_ssm_state",
    )(h0, xdt.reshape(s, inner, 1), dec.reshape(s, inner, 1), r3(xdt), r3(dec), r3(xs), r3(z), r3(bm), r3(cm),
      jnp.repeat(lw['d_skip'], SSM_HEAD_DIM).reshape(1, inner), lw['ssm_norm_g'].reshape(1, inner))


PAGES_PER_STEP = 16


def _page_mean_kernel(pt_ref, *refs, pages_per_block):
    del pt_ref
    pages, o_ref = refs[:-1], refs[-1]
    inv = 1.0 / MOBA_BLOCK
    for r in range(len(pages) // pages_per_block):
        tot = jnp.sum(pages[r * pages_per_block][0, 0], axis=0, keepdims=True)
        for t in range(1, pages_per_block):
            tot = tot + jnp.sum(pages[r * pages_per_block + t][0, 0], axis=0, keepdims=True)
        o_ref[0, 0, r:r + 1, :] = tot * inv


def _paged_block_means(cache_k4, page_table):
    depth, _, page, w = cache_k4.shape
    s, n_pages = page_table.shape
    ppb = MOBA_BLOCK // page
    assert MOBA_BLOCK % page == 0 and n_pages % PAGES_PER_STEP == 0 and PAGES_PER_STEP % (ppb * SUBLANES) == 0
    bps = PAGES_PER_STEP // ppb

    def page_spec(r):
        return pl.BlockSpec((1, 1, page, w), lambda l, b, c, pt: (l, pt[b * n_pages + c * PAGES_PER_STEP + r], 0, 0))

    grid_spec = pltpu.PrefetchScalarGridSpec(
        num_scalar_prefetch=1, grid=(depth, s, n_pages // PAGES_PER_STEP),
        in_specs=[page_spec(r) for r in range(PAGES_PER_STEP)],
        out_specs=pl.BlockSpec((1, 1, bps, w), lambda l, b, c, pt: (l, b, c, 0)))
    return pl.pallas_call(
        functools.partial(_page_mean_kernel, pages_per_block=ppb), grid_spec=grid_spec,
        out_shape=jax.ShapeDtypeStruct((depth, s, n_pages // ppb, w), F32),
        compiler_params=_cparams(("parallel", "parallel", "arbitrary")), name="paged_block_means",
    )(page_table.reshape(-1), *([cache_k4] * PAGES_PER_STEP))


def _gate_topk_kernel(q_ref, km_ref, ind_ref, o_ref):
    km = km_ref[0, 0]
    nb = km.shape[0]
    gate = _dot_f32_by_exact(km * q_ref[0], ind_ref[...])
    blk_id = lax.broadcasted_iota(jnp.int32, gate.shape, 0)
    picks = []
    for _ in range(MOBA_TOPK):
        best = jnp.max(gate, axis=0, keepdims=True)
        idx = jnp.min(jnp.where(gate == best, blk_id, nb), axis=0, keepdims=True)
        picks.append(idx)
        gate = jnp.where(blk_id == idx, -jnp.inf, gate)
    picks.append(jnp.zeros((SUBLANES - MOBA_TOPK, LANES), jnp.int32))
    o_ref[0] = jnp.concatenate(picks, axis=0)


def _gate_topk(q, kmean_all, layer):
    s, w = q.shape
    nb = kmean_all.shape[2]
    n_heads = w // HEAD_DIM
    assert nb >= MOBA_TOPK
    ind = np.zeros((w, LANES), np.float32)
    for h in range(n_heads):
        ind[h * HEAD_DIM:(h + 1) * HEAD_DIM, h] = 1.0
    out = pl.pallas_call(
        _gate_topk_kernel, grid=(s,),
        in_specs=[pl.BlockSpec((1, 1, w), lambda b: (b, 0, 0)),
                  pl.BlockSpec((1, 1, nb, w), lambda b: (layer, b, 0, 0)), _resident((w, LANES))],
        out_specs=pl.BlockSpec((1, SUBLANES, LANES), lambda b: (b, 0, 0)),
        out_shape=jax.ShapeDtypeStruct((s, SUBLANES, LANES), jnp.int32),
        compiler_params=_cparams(("parallel",)), name="decode_gate_topk",
    )(q.reshape(s, 1, w), kmean_all, jnp.asarray(ind, dtype=BF16))
    return out[:, :MOBA_TOPK, :n_heads]


def _moba_decode_kernel(pt_ref, pick_ref, q_ref, kn_ref, vn_ref, *refs, n_pages_sel):
    del pt_ref, pick_ref
    o_ref = refs[-1]
    pages = refs[:-1]
    per_head = 2 * n_pages_sel
    lane = lax.broadcasted_iota(jnp.int32, (1, LANES), 1)
    q = q_ref[0]
    kn = kn_ref[0]
    vn = vn_ref[0]
    outs = []
    for h in range(HEADS_PER_LANE_TILE):
        qh = jnp.where((lane // HEAD_DIM) == h, q, 0.0)
        kp = pages[h * per_head:h * per_head + n_pages_sel]
        vp = pages[h * per_head + n_pages_sel:(h + 1) * per_head]
        s_own = jnp.sum(kn * qh, axis=-1, keepdims=True)
        ss = [jnp.sum(kr[0, 0] * qh, axis=-1, keepdims=True) for kr in kp]
        m = s_own
        for s in ss:
            m = jnp.maximum(m, jnp.max(s, axis=0, keepdims=True))
        p_own = jnp.exp(s_own - m)
        den = p_own
        acc = p_own * vn
        for s, vr in zip(ss, vp):
            p = jnp.exp(s - m)
            den = den + jnp.sum(p, axis=0, keepdims=True)
            acc = acc + jnp.sum(p * vr[0, 0], axis=0, keepdims=True)
        outs.append(acc / den)
    o_ref[0] = jnp.where(lane < HEAD_DIM, outs[0], outs[1])


def _moba_decode(q, k_new, v_new, cache_k4, cache_v4, page_table, picks, layer):
    s, w = q.shape
    page = cache_k4.shape[2]
    n_pages = page_table.shape[1]
    ppb = MOBA_BLOCK // page
    n_heads = w // HEAD_DIM
    n_sel = MOBA_TOPK * ppb

    def page_spec(h_in_pair, r, t):
        def imap(b, hp, pt, pk):
            blk = pk[(b * MOBA_TOPK + r) * n_heads + hp * HEADS_PER_LANE_TILE + h_in_pair]
            return (layer, pt[b * n_pages + blk * ppb + t], 0, hp)
        return pl.BlockSpec((1, 1, page, LANES), imap)

    vec = pl.BlockSpec((1, 1, LANES), lambda b, hp, pt, pk: (b, 0, hp))
    in_specs, operands = [vec, vec, vec], []
    for h in range(HEADS_PER_LANE_TILE):
        for cache in (cache_k4, cache_v4):
            for r in range(MOBA_TOPK):
                for t in range(ppb):
                    in_specs.append(page_spec(h, r, t))
                    operands.append(cache)
    grid_spec = pltpu.PrefetchScalarGridSpec(
        num_scalar_prefetch=2, grid=(s, w // LANES), in_specs=in_specs, out_specs=vec)
    r3 = lambda a: a.reshape(s, 1, w)
    out = pl.pallas_call(
        functools.partial(_moba_decode_kernel, n_pages_sel=n_sel), grid_spec=grid_spec,
        out_shape=jax.ShapeDtypeStruct((s, 1, w), F32),
        compiler_params=_cparams(("parallel", "arbitrary")), name="moba_decode",
    )(page_table.reshape(-1), picks.reshape(-1), r3(q), r3(k_new), r3(v_new), *operands)
    return out.reshape(s, w)


PROJ_TILE = 256
CONV_TILE = 256
SSD_TILE = 256
MERGE_TILE = 512
FFN_TILE = 512


def _prep_layer_weights(lw, dims):
    d_model, cw, aw, sw, xw, n_ssm_heads = dims
    w_in = lw['w_in']
    main = 2 * cw + 3 * aw + sw + xw
    n_heads = aw // HEAD_DIM
    hidden = lw['w_ffn_out'].shape[0]
    proj = {
        'g1': lw['norm1_g'].reshape(1, d_model),
        'wm': w_in[:, :main].astype(BF16),
        'wdt': jnp.pad(w_in[:, main:main + n_ssm_heads], ((0, 0), (0, LANES - n_ssm_heads))).astype(BF16),
        'wg': w_in[:, main + n_ssm_heads:].astype(BF16),
        'qg': jnp.tile(lw['q_norm_g'], n_heads).reshape(1, aw),
        'kg': jnp.tile(lw['k_norm_g'], n_heads).reshape(1, aw),
        'pm': _head_mean_matrix(aw),
    }
    merge = {'wc': lw['w_branch_conv'].astype(BF16), 'wa': lw['w_branch_att'].astype(BF16),
             'ws': lw['w_branch_ssm'].astype(BF16), 'wo': lw['w_out'].astype(BF16)}
    ffn = {'g2': lw['norm2_g'].reshape(1, d_model), 'wa': lw['w_ffn_in'][:, :hidden].astype(BF16),
           'wg': lw['w_ffn_in'][:, hidden:].astype(BF16), 'wo': lw['w_ffn_out'].astype(BF16)}
    return proj, merge, ffn


def kernel(x_prompt, x_sample, cache_k, cache_v, state_conv, state_ssm_conv, state_ssm, page_table,
           norm1_g, w_in, conv_w, conv_b, conv_ln_g, conv_ln_b, q_norm_g, k_norm_g, ssm_conv_w, ssm_conv_b,
           dt_bias, a_log, d_skip, ssm_norm_g, w_branch_conv, w_branch_att, w_branch_ssm, w_out, norm2_g,
           w_ffn_in, w_ffn_out):
    bp, lp, d_model = x_prompt.shape
    bs, ls, _ = x_sample.shape
    depth, n_pool, page, att_heads, head_dim = cache_k.shape
    assert head_dim == HEAD_DIM and ls == 1
    cw = conv_w.shape[-1]
    aw = att_heads * HEAD_DIM
    sw = ssm_norm_g.shape[-1]
    xw = ssm_conv_w.shape[-1]
    n_ssm_heads = a_log.shape[-1]
    assert xw == sw + 2 * SSM_GROUPS * SSM_STATE and sw == n_ssm_heads * SSM_HEAD_DIM
    assert conv_w.shape[1] == CONV_WIDTH and ssm_conv_w.shape[1] == SSM_CONV
    dims = (d_model, cw, aw, sw, xw, n_ssm_heads)
    past_len = page_table.shape[1] * page
    assert past_len % MOBA_BLOCK == 0 and lp % PROJ_TILE == 0

    cos_p, sin_p = _rope_tables(jnp.arange(lp, dtype=jnp.int32), att_heads)
    cos_s, sin_s = _rope_tables(jnp.full((bs,), past_len, jnp.int32), att_heads)
    cache_k4 = cache_k.reshape(depth, n_pool, page, aw)
    cache_v4 = cache_v.reshape(depth, n_pool, page, aw)
    kmean_past = _paged_block_means(cache_k4, page_table)

    hp = x_prompt.reshape(bp * lp, d_model)
    hs = x_sample.reshape(bs, d_model)
    outs = {name: [] for name in ('kp', 'vp', 'cp', 'scp', 'sp', 'ks', 'vs', 'cs', 'scs', 'ss')}
    for l in range(depth):
        lw = {'w_in': w_in[l], 'norm1_g': norm1_g[l], 'q_norm_g': q_norm_g[l], 'k_norm_g': k_norm_g[l],
              'conv_w': conv_w[l], 'conv_b': conv_b[l], 'conv_ln_g': conv_ln_g[l], 'conv_ln_b': conv_ln_b[l],
              'ssm_conv_w': ssm_conv_w[l], 'ssm_conv_b': ssm_conv_b[l], 'dt_bias': dt_bias[l],
              'a_log': a_log[l], 'd_skip': d_skip[l], 'ssm_norm_g': ssm_norm_g[l],
              'w_branch_conv': w_branch_conv[l], 'w_branch_att': w_branch_att[l],
              'w_branch_ssm': w_branch_ssm[l], 'w_out': w_out[l], 'norm2_g': norm2_g[l],
              'w_ffn_in': w_ffn_in[l], 'w_ffn_out': w_ffn_out[l]}
        pw, mw, fw = _prep_layer_weights(lw, dims)

        u, q, k, v, z, xbc, dt_raw, gates, k_b, v_b, kmean = _project(
            hp, pw, cos_p, sin_p, dims, PROJ_TILE, lp // PROJ_TILE, block_stats=True)
        seq = lambda a: a.reshape(bp, lp, a.shape[-1])
        o_conv = _conv_branch_seq(seq(u), lw, CONV_TILE)
        o_att = _moba_seq(seq(q), seq(k_b), seq(v_b), kmean.reshape(bp, lp // MOBA_BLOCK, aw))
        o_ssm, h_fin = _ssd_seq(seq(xbc), seq(dt_raw), seq(z), lw, SSD_TILE)
        flat = lambda a: a.reshape(bp * lp, a.shape[-1])
        hp = _merge(hp, flat(o_conv), flat(o_att), flat(o_ssm), gates, mw, MERGE_TILE)
        hp = _ffn(hp, fw, FFN_TILE)
        outs['kp'].append(k.reshape(bp, lp, att_heads, HEAD_DIM))
        outs['vp'].append(v.reshape(bp, lp, att_heads, HEAD_DIM))
        outs['cp'].append(seq(u)[:, lp - (CONV_WIDTH - 1):])
        outs['scp'].append(seq(xbc)[:, lp - (SSM_CONV - 1):])
        outs['sp'].append(h_fin.reshape(bp, n_ssm_heads, SSM_HEAD_DIM, SSM_STATE))

        u, q, k, v, z, xbc, dt_raw, gates = _project(hs, pw, cos_s, sin_s, dims, bs, 1, block_stats=False)
        o_conv, xs, bm, cm, xdt, dec = _decode_mix(u, state_conv[l], xbc, state_ssm_conv[l], dt_raw, lw, sw)
        h_new, o_ssm = _decode_state(state_ssm[l].reshape(bs, sw, SSM_STATE), xdt, dec, xs, z, bm, cm, lw, sw)
        picks = _gate_topk(q, kmean_past, l)
        o_att = _moba_decode(q, k, v, cache_k4, cache_v4, page_table, picks, l)
        hs = _merge(hs, o_conv, o_att.astype(BF16), o_ssm.reshape(bs, sw), gates, mw, bs)
        hs = _ffn(hs, fw, bs)
        outs['ks'].append(k.reshape(bs, 1, att_heads, HEAD_DIM))
        outs['vs'].append(v.reshape(bs, 1, att_heads, HEAD_DIM))
        outs['cs'].append(jnp.concatenate([state_conv[l][:, 1:], u[:, None, :]], axis=1))
        outs['scs'].append(jnp.concatenate([state_ssm_conv[l][:, 1:], xbc[:, None, :]], axis=1))
        outs['ss'].append(h_new.reshape(bs, n_ssm_heads, SSM_HEAD_DIM, SSM_STATE))

    st = lambda name: jnp.stack(outs[name])
    return (hp.reshape(bp, lp, d_model), hs.reshape(bs, 1, d_model),
            st('kp'), st('vp'), st('cp'), st('scp'), st('sp'),
            st('ks'), st('vs'), st('cs'), st('scs'), st('ss'))
```

```python
import functools

import numpy as np
import jax
import jax.numpy as jnp
from jax import lax
from jax.experimental import pallas as pl
from jax.experimental.pallas import tpu as pltpu

F32 = jnp.float32
BF16 = jnp.bfloat16

CONV_WIDTH = 31
HEAD_DIM = 64
MOBA_BLOCK = 256
MOBA_TOPK = 3
ROPE_THETA = 10000.0
SSM_HEAD_DIM = 64
SSM_GROUPS = 2
SSM_STATE = 128
SSM_CONV = 4
SSD_CHUNK = 128
NORM_EPS = 1e-6

LANES = 128
SUBLANES = 8
VMEM_LIMIT_BYTES = 56 * 1024 * 1024

NEG_BIG = -1e30
LOG2_E = 1.4426950408889634
HEADS_PER_LANE_TILE = LANES // HEAD_DIM


def _cparams(sem=None):
    return pltpu.CompilerParams(dimension_semantics=sem, vmem_limit_bytes=VMEM_LIMIT_BYTES)


def _resident(shape):
    nd = len(shape)
    return pl.BlockSpec(shape, lambda *_: (0,) * nd, pipeline_mode=pl.Buffered(1))


def _split3(x):
    hi = x.astype(BF16)
    r1 = x - hi.astype(F32)
    mid = r1.astype(BF16)
    lo = (r1 - mid.astype(F32)).astype(BF16)
    return hi, mid, lo


def _dot_nt(a, b):
    return lax.dot_general(a, b, (((1,), (1,)), ((), ())), preferred_element_type=F32)


def _dot_f32_by_exact(a, b_bf16, nt=False):
    f = (lambda p: _dot_nt(p, b_bf16)) if nt else (lambda p: jnp.dot(p, b_bf16, preferred_element_type=F32))
    hi, mid, lo = _split3(a)
    return f(hi) + f(mid) + f(lo)


def _sigmoid(x):
    return 1.0 / (1.0 + jnp.exp(-x))


def _silu(x):
    return x * _sigmoid(x)


def _softplus(x):
    return jnp.maximum(x, 0.0) + jnp.log(1.0 + jnp.exp(-jnp.abs(x)))


def _rmsnorm_rows(x, gain):
    return (x * lax.rsqrt(jnp.mean(x * x, axis=-1, keepdims=True) + NORM_EPS)) * gain


def _proj_kernel(x_ref, g1_ref, wm_ref, wdt_ref, wg_ref, cos_ref, sin_ref, qg_ref, kg_ref, pm_ref,
                 u_ref, q_ref, k_ref, v_ref, z_ref, xbc_ref, dt_ref, gate_ref, *block_refs,
                 widths, blocks_per_tile, q_scale):
    cw, aw, sw, xw = widths
    x = x_ref[...]
    hn = _rmsnorm_rows(x, g1_ref[...]).astype(BF16)

    def proj(lo, width):
        return jnp.dot(hn, wm_ref[:, lo:lo + width], preferred_element_type=F32)

    off = 0
    a = proj(off, cw)
    g = proj(off + cw, cw)
    u_ref[...] = a * _sigmoid(g)
    off += 2 * cw

    lane = lax.broadcasted_iota(jnp.int32, (x.shape[0], aw), 1)
    first_half = (lane % HEAD_DIM) < (HEAD_DIM // 2)
    cos = cos_ref[...]
    sin = sin_ref[...]

    def qk_norm_rope(t, gain_ref):
        ss = jnp.dot((t * t).astype(BF16), pm_ref[...], preferred_element_type=F32)
        tn = (t * lax.rsqrt(ss + NORM_EPS)) * gain_ref[...]
        up = pltpu.roll(tn, aw - HEAD_DIM // 2, axis=1)
        dn = pltpu.roll(tn, HEAD_DIM // 2, axis=1)
        return tn * cos + jnp.where(first_half, up, dn) * sin

    q = qk_norm_rope(proj(off, aw), qg_ref)
    q_ref[...] = (q * q_scale).astype(q_ref.dtype)
    off += aw
    k = qk_norm_rope(proj(off, aw), kg_ref)
    k_ref[...] = k
    off += aw
    v = proj(off, aw)
    v_ref[...] = v
    off += aw
    z_ref[...] = proj(off, sw)
    off += sw
    xbc_ref[...] = proj(off, xw)
    dt_ref[...] = jnp.dot(hn, wdt_ref[...], preferred_element_type=F32)
    gate_ref[...] = _sigmoid(jnp.dot(hn, wg_ref[...], preferred_element_type=F32)).astype(BF16)
    if blocks_per_tile:
        kb_ref, vb_ref, km_ref = block_refs
        kb_ref[...] = k.astype(BF16)
        vb_ref[...] = v.astype(BF16)
        for c in range(blocks_per_tile):
            km_ref[0, c:c + 1, :] = jnp.mean(k[c * MOBA_BLOCK:(c + 1) * MOBA_BLOCK], axis=0, keepdims=True)


def _rope_tables(pos, n_heads):
    half = HEAD_DIM // 2
    inv = jnp.power(ROPE_THETA, -jnp.arange(half, dtype=F32) * (2.0 / HEAD_DIM))
    ang = pos.astype(F32)[:, None] * inv[None, :]
    cos = jnp.cos(ang)
    sin = jnp.sin(ang)
    cos_h = jnp.concatenate([cos, cos], axis=-1)
    sin_h = jnp.concatenate([-sin, sin], axis=-1)
    return jnp.tile(cos_h, (1, n_heads)), jnp.tile(sin_h, (1, n_heads))


def _head_mean_matrix(width):
    idx = np.arange(width) // HEAD_DIM
    return jnp.asarray((idx[:, None] == idx[None, :]).astype(np.float32) / HEAD_DIM, dtype=BF16)


def _project(x2d, pw, cos, sin, dims, tile_rows, seq_tiles, block_stats):
    d_model, cw, aw, sw, xw, _ = dims
    n = x2d.shape[0]
    tm = tile_rows
    assert n % tm == 0
    bpt = tm // MOBA_BLOCK if block_stats else 0
    assert not block_stats or tm % MOBA_BLOCK == 0
    n_gate = pw['wg'].shape[1]
    row = lambda w: pl.BlockSpec((tm, w), lambda i: (i, 0))
    tab = pl.BlockSpec((tm, aw), lambda i: (i % seq_tiles, 0))
    in_specs = [row(d_model), _resident((1, d_model)), _resident(pw['wm'].shape), _resident(pw['wdt'].shape),
                _resident(pw['wg'].shape), tab, tab, _resident((1, aw)), _resident((1, aw)), _resident((aw, aw))]
    sds = jax.ShapeDtypeStruct
    out_shape = [sds((n, cw), F32), sds((n, aw), BF16 if block_stats else F32), sds((n, aw), F32),
                 sds((n, aw), F32), sds((n, sw), F32), sds((n, xw), F32), sds((n, LANES), F32),
                 sds((n, n_gate), BF16)]
    out_specs = [row(cw), row(aw), row(aw), row(aw), row(sw), row(xw), row(LANES), row(n_gate)]
    if block_stats:
        out_shape += [sds((n, aw), BF16), sds((n, aw), BF16), sds((n // tm, bpt, aw), F32)]
        out_specs += [row(aw), row(aw), pl.BlockSpec((1, bpt, aw), lambda i: (i, 0, 0))]
    q_scale = HEAD_DIM ** -0.5 * (LOG2_E if block_stats else 1.0)
    return pl.pallas_call(
        functools.partial(_proj_kernel, widths=(cw, aw, sw, xw), blocks_per_tile=bpt, q_scale=q_scale),
        grid=(n // tm,), in_specs=in_specs, out_specs=out_specs, out_shape=out_shape,
        compiler_params=_cparams(("parallel",)), name="in_proj",
    )(x2d, pw['g1'], pw['wm'], pw['wdt'], pw['wg'], cos, sin, pw['qg'], pw['kg'], pw['pm'])


CONV_HALO = 32


def _conv_seq_kernel(u_ref, w_ref, b_ref, lg_ref, lb_ref, o_ref, ext_ref, *, tile):
    @pl.when(pl.program_id(1) == 0)
    def _():
        ext_ref[0:CONV_HALO, :] = jnp.zeros((CONV_HALO, ext_ref.shape[1]), F32)

    ext_ref[CONV_HALO:CONV_HALO + tile, :] = u_ref[0]
    base = CONV_HALO - (CONV_WIDTH - 1)
    acc = ext_ref[base:base + tile, :] * w_ref[0:1, :]
    for kk in range(1, CONV_WIDTH):
        acc = acc + ext_ref[base + kk:base + kk + tile, :] * w_ref[kk:kk + 1, :]
    y = acc + b_ref[...]
    mu = jnp.mean(y, axis=-1, keepdims=True)
    yc = y - mu
    var = jnp.mean(yc * yc, axis=-1, keepdims=True)
    yn = (yc * lax.rsqrt(var + NORM_EPS)) * lg_ref[...] + lb_ref[...]
    o_ref[0] = _silu(yn).astype(o_ref.dtype)
    ext_ref[0:CONV_HALO, :] = ext_ref[tile:tile + CONV_HALO, :]


def _conv_branch_seq(u, lw, tile):
    bsz, L, c = u.shape
    assert L % tile == 0 and tile >= CONV_HALO
    vec = lambda a: a.reshape(1, c)
    return pl.pallas_call(
        functools.partial(_conv_seq_kernel, tile=tile),
        grid=(bsz, L // tile),
        in_specs=[pl.BlockSpec((1, tile, c), lambda b, l: (b, l, 0)), _resident((CONV_WIDTH, c)),
                  _resident((1, c)), _resident((1, c)), _resident((1, c))],
        out_specs=pl.BlockSpec((1, tile, c), lambda b, l: (b, l, 0)),
        out_shape=jax.ShapeDtypeStruct((bsz, L, c), BF16),
        scratch_shapes=[pltpu.VMEM((CONV_HALO + tile, c), F32)],
        compiler_params=_cparams(("parallel", "arbitrary")), name="conv_branch",
    )(u, lw['conv_w'], vec(lw['conv_b']), vec(lw['conv_ln_g']), vec(lw['conv_ln_b']))


def _moba_seq_kernel(q_ref, k_ref, v_ref, km_ref, blockmask_ref, o_ref, *, n_blocks, chunk_blocks):
    i = pl.program_id(2)
    blk = MOBA_BLOCK
    ck = chunk_blocks * blk
    nh = HEADS_PER_LANE_TILE
    q_pair = q_ref[0]
    lane = lax.broadcasted_iota(jnp.int32, (blk, LANES), 1)
    blk_id = lax.broadcasted_iota(jnp.int32, (n_blocks, blk), 0)
    km_parts = _split3(km_ref[0])

    q_aug = []
    for h in range(nh):
        qh = jnp.where((lane // HEAD_DIM) == h, q_pair, jnp.zeros_like(q_pair))
        gate = _dot_nt(km_parts[0], qh) + _dot_nt(km_parts[1], qh) + _dot_nt(km_parts[2], qh)
        gate = jnp.where(blk_id < i, gate, -jnp.inf)
        unpicked = jnp.ones((n_blocks, blk), F32)
        for _ in range(MOBA_TOPK):
            best = jnp.max(gate, axis=0, keepdims=True)
            idx = jnp.min(jnp.where(gate == best, blk_id, n_blocks), axis=0, keepdims=True)
            hit = blk_id == idx
            unpicked = jnp.where(hit, 0.0, unpicked)
            gate = jnp.where(hit, -jnp.inf, gate)
        unpicked = jnp.concatenate([unpicked, jnp.zeros((LANES - n_blocks, blk), F32)], axis=0)
        q_aug.append(jnp.concatenate([qh, unpicked.T.astype(BF16)], axis=1))

    qa = jnp.concatenate(q_aug, axis=0)
    own = pl.multiple_of(i * blk, blk)
    row = lax.broadcasted_iota(jnp.int32, (nh * blk, blk), 0) % blk
    col = lax.broadcasted_iota(jnp.int32, (nh * blk, blk), 1)
    s = jnp.where(col <= row, _dot_nt(qa[:, 0:LANES], k_ref[0, pl.ds(own, blk), :]), NEG_BIG)
    m = jnp.max(s, axis=-1, keepdims=True)
    p = jnp.exp2(s - m)
    l = jnp.sum(p, axis=-1, keepdims=True)
    acc = jnp.dot(p.astype(BF16), v_ref[0, pl.ds(own, blk), :], preferred_element_type=F32)

    def body(c, carry):
        m, l, acc = carry
        start = pl.multiple_of(c * ck, ck)
        k_aug = jnp.concatenate([k_ref[0, pl.ds(start, ck), :], blockmask_ref[pl.ds(start, ck), :]], axis=1)
        s = _dot_nt(qa, k_aug)
        m_new = jnp.maximum(m, jnp.max(s, axis=-1, keepdims=True))
        alpha = jnp.exp2(m - m_new)
        p = jnp.exp2(s - m_new)
        l = l * alpha + jnp.sum(p, axis=-1, keepdims=True)
        acc = acc * alpha + jnp.dot(p.astype(BF16), v_ref[0, pl.ds(start, ck), :], preferred_element_type=F32)
        return m_new, l, acc

    n_chunks = (i + chunk_blocks - 1) // chunk_blocks
    m, l, acc = lax.fori_loop(0, n_chunks, body, (m, l, acc))
    out = acc / l
    o_ref[0] = jnp.where(lane < HEAD_DIM, out[0:blk], out[blk:2 * blk]).astype(o_ref.dtype)


MOBA_CHUNK_BLOCKS = 4


def _moba_seq(q, k, v, kmean):
    bsz, L, w = q.shape
    nb = L // MOBA_BLOCK
    assert L % MOBA_BLOCK == 0 and w % LANES == 0 and nb <= LANES and HEADS_PER_LANE_TILE == 2
    cb = MOBA_CHUNK_BLOCKS if nb % MOBA_CHUNK_BLOCKS == 0 else 1
    blockmask = np.zeros((L, LANES), np.float32)
    blockmask[np.arange(L), np.arange(L) // MOBA_BLOCK] = NEG_BIG
    kv_spec = pl.BlockSpec((1, L, LANES), lambda b, hp, i: (b, 0, hp))
    return pl.pallas_call(
        functools.partial(_moba_seq_kernel, n_blocks=nb, chunk_blocks=cb),
        grid=(bsz, w // LANES, nb),
        in_specs=[pl.BlockSpec((1, MOBA_BLOCK, LANES), lambda b, hp, i: (b, i, hp)), kv_spec, kv_spec,
                  pl.BlockSpec((1, nb, LANES), lambda b, hp, i: (b, 0, hp)), _resident((L, LANES))],
        out_specs=pl.BlockSpec((1, MOBA_BLOCK, LANES), lambda b, hp, i: (b, i, hp)),
        out_shape=jax.ShapeDtypeStruct((bsz, L, w), BF16),
        compiler_params=_cparams(("parallel", "parallel", "arbitrary")), name="moba_prompt",
    )(q, k, v, kmean, jnp.asarray(blockmask, dtype=BF16))


SSM_HALO = SUBLANES


def _ssd_seq_kernel(xbc_ref, dt_ref, z_ref, cw_ref, cb_ref, dtb_ref, a_ref, dsk_ref, ng_ref, tri_ref,
                    o_ref, h_ref, ext_ref, y_ref, *, tile, inner, n_pairs):
    q = SSD_CHUNK
    gs = SSM_GROUPS * SSM_STATE

    @pl.when(pl.program_id(1) == 0)
    def _():
        ext_ref[0:SSM_HALO, :] = jnp.zeros((SSM_HALO, ext_ref.shape[1]), F32)
        h_ref[...] = jnp.zeros_like(h_ref)

    ext_ref[SSM_HALO:SSM_HALO + tile, :] = xbc_ref[0]
    base = SSM_HALO - (SSM_CONV - 1)
    acc = ext_ref[base:base + tile, :] * cw_ref[0:1, :]
    for kk in range(1, SSM_CONV):
        acc = acc + ext_ref[base + kk:base + kk + tile, :] * cw_ref[kk:kk + 1, :]
    xc = _silu(acc + cb_ref[...])
    ext_ref[0:SSM_HALO, :] = ext_ref[tile:tile + SSM_HALO, :]

    dt = _softplus(dt_ref[0] + dtb_ref[...])
    da = dt * a_ref[...]
    tri = tri_ref[...]
    rows = lax.broadcasted_iota(jnp.int32, (q, q), 0)
    cols = lax.broadcasted_iota(jnp.int32, (q, q), 1)
    lower = cols <= rows
    lane = lax.broadcasted_iota(jnp.int32, (q, LANES), 1)
    left = lane < SSM_HEAD_DIM
    top = lax.broadcasted_iota(jnp.int32, (LANES, SSM_STATE), 0) < SSM_HEAD_DIM

    for c in range(tile // q):
        r0 = c * q
        hi, mid, lo = _split3(da[r0:r0 + q])
        cum = (jnp.dot(tri, hi, preferred_element_type=F32) + jnp.dot(tri, mid, preferred_element_type=F32)
               + jnp.dot(tri, lo, preferred_element_type=F32))
        cum_t = cum.T
        e_cum = jnp.exp(cum)
        e_rest = jnp.exp(cum[q - 1:q, :] - cum)
        e_last_t = jnp.exp(cum_t[:, q - 1:q])
        dt_c = dt[r0:r0 + q]
        bm = xc[r0:r0 + q, inner:inner + gs]
        cm = xc[r0:r0 + q, inner + gs:inner + 2 * gs]
        for p in range(n_pairs):
            g = (p * HEADS_PER_LANE_TILE * SSM_HEAD_DIM) // (inner // SSM_GROUPS)
            b_g = bm[:, g * SSM_STATE:(g + 1) * SSM_STATE].astype(BF16)
            c_g = cm[:, g * SSM_STATE:(g + 1) * SSM_STATE].astype(BF16)
            cb = _dot_nt(c_g, b_g)
            h0, h1 = HEADS_PER_LANE_TILE * p, HEADS_PER_LANE_TILE * p + 1
            xs = xc[r0:r0 + q, p * LANES:(p + 1) * LANES]
            xdt = xs * jnp.where(left, dt_c[:, h0:h0 + 1], dt_c[:, h1:h1 + 1])
            xdt_b = xdt.astype(BF16)
            y_pair = None
            for hh, head in enumerate((h0, h1)):
                seg = cum[:, head:head + 1] - cum_t[head:head + 1, :]
                lmat = jnp.where(lower, jnp.exp(seg), 0.0)
                y_h = jnp.dot((cb * lmat).astype(BF16), xdt_b, preferred_element_type=F32)
                y_pair = y_h if hh == 0 else jnp.where(left, y_pair, y_h)
            h_pair = h_ref[0, p * LANES:(p + 1) * LANES, :]
            inter = _dot_nt(c_g, h_pair.astype(BF16))
            y_pair = y_pair + inter * jnp.where(left, e_cum[:, h0:h0 + 1], e_cum[:, h1:h1 + 1])
            y_ref[r0:r0 + q, p * LANES:(p + 1) * LANES] = y_pair
            xdd = xdt * jnp.where(left, e_rest[:, h0:h0 + 1], e_rest[:, h1:h1 + 1])
            upd = jnp.dot(xdd.T.astype(BF16), b_g, preferred_element_type=F32)
            keep = jnp.where(top, e_last_t[h0:h0 + 1, :], e_last_t[h1:h1 + 1, :])
            h_ref[0, p * LANES:(p + 1) * LANES, :] = h_pair * keep + upd

    y = y_ref[...] + dsk_ref[...] * xc[:, 0:inner]
    gated = y * _silu(z_ref[0])
    o_ref[0] = _rmsnorm_rows(gated, ng_ref[...]).astype(o_ref.dtype)


def _ssd_seq(xbc, dt_raw, z, lw, tile):
    bsz, L, xw = xbc.shape
    inner = z.shape[-1]
    n_heads = inner // SSM_HEAD_DIM
    assert L % tile == 0 and tile % SSD_CHUNK == 0 and inner % LANES == 0
    assert (inner // SSM_GROUPS) % LANES == 0
    pad_h = lambda a: jnp.pad(a, (0, LANES - n_heads)).reshape(1, LANES)
    tri = jnp.asarray(np.tril(np.ones((SSD_CHUNK, SSD_CHUNK), np.float32)), dtype=BF16)
    seq = lambda w: pl.BlockSpec((1, tile, w), lambda b, l: (b, l, 0))
    return pl.pallas_call(
        functools.partial(_ssd_seq_kernel, tile=tile, inner=inner, n_pairs=inner // LANES),
        grid=(bsz, L // tile),
        in_specs=[seq(xw), seq(LANES), seq(inner), _resident((SSM_CONV, xw)), _resident((1, xw)),
                  _resident((1, LANES)), _resident((1, LANES)), _resident((1, inner)), _resident((1, inner)),
                  _resident((SSD_CHUNK, SSD_CHUNK))],
        out_specs=[seq(inner), pl.BlockSpec((1, inner, SSM_STATE), lambda b, l: (b, 0, 0))],
        out_shape=[jax.ShapeDtypeStruct((bsz, L, inner), BF16),
                   jax.ShapeDtypeStruct((bsz, inner, SSM_STATE), F32)],
        scratch_shapes=[pltpu.VMEM((SSM_HALO + tile, xw), F32), pltpu.VMEM((tile, inner), F32)],
        compiler_params=_cparams(("parallel", "arbitrary")), name="ssd_scan",
    )(xbc, dt_raw, z, lw['ssm_conv_w'], lw['ssm_conv_b'].reshape(1, xw), pad_h(lw['dt_bias']),
      pad_h(-jnp.exp(lw['a_log'])), jnp.repeat(lw['d_skip'], SSM_HEAD_DIM).reshape(1, inner),
      lw['ssm_norm_g'].reshape(1, inner), tri)


def _merge_kernel(x_ref, oc_ref, oa_ref, os_ref, gate_ref, wc_ref, wa_ref, ws_ref, wo_ref, o_ref):
    d = x_ref.shape[1]
    gates = gate_ref[...]
    branch = lambda o, w: jnp.dot(o[...], w[...], preferred_element_type=F32)
    merged = (gates[:, 0:d].astype(F32) * branch(oc_ref, wc_ref)
              + gates[:, d:2 * d].astype(F32) * branch(oa_ref, wa_ref)
              + gates[:, 2 * d:3 * d].astype(F32) * branch(os_ref, ws_ref))
    o_ref[...] = x_ref[...] + jnp.dot(merged.astype(BF16), wo_ref[...], preferred_element_type=F32)


def _merge(x2d, o_conv, o_att, o_ssm, gates, mw, tile_rows):
    n, d = x2d.shape
    tm = tile_rows
    assert n % tm == 0
    row = lambda w: pl.BlockSpec((tm, w), lambda i: (i, 0))
    bw = o_conv.shape[1]
    return pl.pallas_call(
        _merge_kernel, grid=(n // tm,),
        in_specs=[row(d), row(bw), row(bw), row(bw), row(3 * d), _resident((bw, d)), _resident((bw, d)),
                  _resident((bw, d)), _resident((d, d))],
        out_specs=row(d), out_shape=jax.ShapeDtypeStruct((n, d), F32),
        compiler_params=_cparams(("parallel",)), name="merge_out_proj",
    )(x2d, o_conv, o_att, o_ssm, gates, mw['wc'], mw['wa'], mw['ws'], mw['wo'])


def _ffn_kernel(x_ref, g2_ref, wa_ref, wg_ref, wo_ref, o_ref, *, chunk):
    x = x_ref[...]
    hn = _rmsnorm_rows(x, g2_ref[...]).astype(BF16)
    acc = x
    for c in range(wa_ref.shape[1] // chunk):
        sl = slice(c * chunk, (c + 1) * chunk)
        a = jnp.dot(hn, wa_ref[:, sl], preferred_element_type=F32)
        g = jnp.dot(hn, wg_ref[:, sl], preferred_element_type=F32)
        acc = acc + jnp.dot((_silu(a) * g).astype(BF16), wo_ref[sl, :], preferred_element_type=F32)
    o_ref[...] = acc


def _ffn_chunk(hidden):
    best = LANES
    for c in range(LANES, hidden + 1, LANES):
        if hidden % c == 0 and c <= 1536:
            best = c
    return best


def _ffn(x2d, fw, tile_rows):
    n, d = x2d.shape
    tm = tile_rows
    hidden = fw['wa'].shape[1]
    assert n % tm == 0 and hidden % LANES == 0
    row = pl.BlockSpec((tm, d), lambda i: (i, 0))
    return pl.pallas_call(
        functools.partial(_ffn_kernel, chunk=_ffn_chunk(hidden)), grid=(n // tm,),
        in_specs=[row, _resident((1, d)), _resident((d, hidden)), _resident((d, hidden)), _resident((hidden, d))],
        out_specs=row, out_shape=jax.ShapeDtypeStruct((n, d), F32),
        compiler_params=_cparams(("parallel",)), name="swiglu_ffn",
    )(x2d, fw['g2'], fw['wa'], fw['wg'], fw['wo'])


def _decode_mix_kernel(u_ref, cst_ref, cw_ref, cb_ref, lg_ref, lb_ref, xbc_ref, sst_ref, sw_ref, sb_ref,
                       dt_ref, dtb_ref, a_ref, ex_ref,
                       oc_ref, xs_ref, bm_ref, cm_ref, xdt_ref, dec_ref, *, inner):
    gs = SSM_GROUPS * SSM_STATE
    acc = u_ref[...] * cw_ref[CONV_WIDTH - 1:CONV_WIDTH, :] + cb_ref[...]
    for kk in range(CONV_WIDTH - 1):
        acc = acc + cst_ref[kk] * cw_ref[kk:kk + 1, :]
    mu = jnp.mean(acc, axis=-1, keepdims=True)
    yc = acc - mu
    var = jnp.mean(yc * yc, axis=-1, keepdims=True)
    oc_ref[...] = _silu((yc * lax.rsqrt(var + NORM_EPS)) * lg_ref[...] + lb_ref[...]).astype(oc_ref.dtype)

    acc = xbc_ref[...] * sw_ref[SSM_CONV - 1:SSM_CONV, :] + sb_ref[...]
    for kk in range(SSM_CONV - 1):
        acc = acc + sst_ref[kk] * sw_ref[kk:kk + 1, :]
    xc = _silu(acc)
    xs = xc[:, 0:inner]
    xs_ref[...] = xs
    bm_ref[...] = xc[:, inner:inner + gs]
    cm_ref[...] = xc[:, inner + gs:inner + 2 * gs]
    dt = _softplus(dt_ref[...] + dtb_ref[...])
    dec = jnp.exp(dt * a_ref[...])
    xdt_ref[...] = xs * _dot_f32_by_exact(dt, ex_ref[...])
    dec_ref[...] = _dot_f32_by_exact(dec, ex_ref[...])


def _decode_mix(u, conv_state, xbc, ssm_conv_state, dt_raw, lw, inner):
    s, cw = u.shape
    xw = xbc.shape[1]
    n_heads = inner // SSM_HEAD_DIM
    gs = SSM_GROUPS * SSM_STATE
    pad_h = lambda a: jnp.pad(a, (0, LANES - n_heads)).reshape(1, LANES)
    expand = np.zeros((LANES, inner), np.float32)
    for h in range(n_heads):
        expand[h, h * SSM_HEAD_DIM:(h + 1) * SSM_HEAD_DIM] = 1.0
    sds = jax.ShapeDtypeStruct
    return pl.pallas_call(
        functools.partial(_decode_mix_kernel, inner=inner),
        out_shape=[sds((s, cw), BF16), sds((s, inner), F32), sds((s, gs), F32), sds((s, gs), F32),
                   sds((s, inner), F32), sds((s, inner), F32)],
        compiler_params=_cparams(), name="decode_conv_ssm_inputs",
    )(u, jnp.swapaxes(conv_state, 0, 1), lw['conv_w'], lw['conv_b'].reshape(1, cw),
      lw['conv_ln_g'].reshape(1, cw), lw['conv_ln_b'].reshape(1, cw),
      xbc, jnp.swapaxes(ssm_conv_state, 0, 1), lw['ssm_conv_w'], lw['ssm_conv_b'].reshape(1, xw),
      dt_raw, pad_h(lw['dt_bias']), pad_h(-jnp.exp(lw['a_log'])), jnp.asarray(expand, dtype=BF16))


def _decode_state_kernel(h_ref, xcol_ref, dcol_ref, xdt_ref, dec_ref, xs_ref, z_ref, bm_ref, cm_ref,
                         dsk_ref, ng_ref, hn_ref, o_ref, *, inner):
    half = inner // SSM_GROUPS
    h = h_ref[0]
    row = lax.broadcasted_iota(jnp.int32, h.shape, 0)
    b0 = bm_ref[0][:, 0:SSM_STATE]
    b1 = bm_ref[0][:, SSM_STATE:2 * SSM_STATE]
    c0 = cm_ref[0][:, 0:SSM_STATE]
    c1 = cm_ref[0][:, SSM_STATE:2 * SSM_STATE]
    hn_ref[0] = h * dcol_ref[0] + xcol_ref[0] * jnp.where(row < half, b0, b1)

    def c_dot_h(c_row, h_rows):
        ch, cm_, _ = _split3(jnp.broadcast_to(c_row, (SUBLANES, SSM_STATE)))
        hh, hm, _ = _split3(h_rows)
        return (_dot_nt(ch, hh) + _dot_nt(ch, hm) + _dot_nt(cm_, hh))[0:1, :]

    ch = jnp.concatenate([c_dot_h(c0, h[0:half]), c_dot_h(c1, h[half:inner])], axis=1)
    lane = lax.broadcasted_iota(jnp.int32, (1, inner), 1)
    cb = jnp.where(lane < half, jnp.sum(c0 * b0, axis=-1, keepdims=True), jnp.sum(c1 * b1, axis=-1, keepdims=True))
    y = ch * dec_ref[0] + cb * xdt_ref[0] + dsk_ref[...] * xs_ref[0]
    gated = y * _silu(z_ref[0])
    o_ref[0] = _rmsnorm_rows(gated, ng_ref[...]).astype(o_ref.dtype)


def _decode_state(h0, xdt, dec, xs, z, bm, cm, lw, inner):
    s = h0.shape[0]
    gs = SSM_GROUPS * SSM_STATE
    per = lambda shape: pl.BlockSpec((1,) + shape, lambda b: (b, 0, 0))
    r3 = lambda a: a.reshape(s, 1, a.shape[-1])
    return pl.pallas_call(
        functools.partial(_decode_state_kernel, inner=inner), grid=(s,),
        in_specs=[per((inner, SSM_STATE)), per((inner, 1)), per((inner, 1)), per((1, inner)), per((1, inner)),
                  per((1, inner)), per((1, inner)), per((1, gs)), per((1, gs)), _resident((1, inner)),
                  _resident((1, inner))],
        out_specs=[per((inner, SSM_STATE)), per((1, inner))],
        out_shape=[jax.ShapeDtypeStruct((s, inner, SSM_STATE), F32), jax.ShapeDtypeStruct((s, 1, inner), BF16)],
        compiler_params=_cparams(("parallel",)), name="decode_ssm_state",
    )(h0, xdt.reshape(s, inner, 1), dec.reshape(s, inner, 1), r3(xdt), r3(dec), r3(xs), r3(z), r3(bm), r3(cm),
      jnp.repeat(lw['d_skip'], SSM_HEAD_DIM).reshape(1, inner), lw['ssm_norm_g'].reshape(1, inner))


PAGES_PER_STEP = 16


def _page_mean_kernel(pt_ref, *refs, pages_per_block):
    del pt_ref
    pages, o_ref = refs[:-1], refs[-1]
    inv = 1.0 / MOBA_BLOCK
    for r in range(len(pages) // pages_per_block):
        tot = jnp.sum(pages[r * pages_per_block][0, 0], axis=0)
        for t in range(1, pages_per_block):
            tot = tot + jnp.sum(pages[r * pages_per_block + t][0, 0], axis=0)
        o_ref[0, 0, r] = tot * inv


def _paged_block_means(cache_k, page_table):
    depth, _, page, n_heads, hd = cache_k.shape
    s, n_pages = page_table.shape
    ppb = MOBA_BLOCK // page
    assert MOBA_BLOCK % page == 0 and n_pages % PAGES_PER_STEP == 0 and PAGES_PER_STEP % ppb == 0
    bps = PAGES_PER_STEP // ppb

    def page_spec(r):
        return pl.BlockSpec((1, 1, page, n_heads, hd),
                            lambda l, b, c, pt: (l, pt[b * n_pages + c * PAGES_PER_STEP + r], 0, 0, 0))

    grid_spec = pltpu.PrefetchScalarGridSpec(
        num_scalar_prefetch=1, grid=(depth, s, n_pages // PAGES_PER_STEP),
        in_specs=[page_spec(r) for r in range(PAGES_PER_STEP)],
        out_specs=pl.BlockSpec((1, 1, bps, n_heads, hd), lambda l, b, c, pt: (l, b, c, 0, 0)))
    return pl.pallas_call(
        functools.partial(_page_mean_kernel, pages_per_block=ppb), grid_spec=grid_spec,
        out_shape=jax.ShapeDtypeStruct((depth, s, n_pages // ppb, n_heads, hd), F32),
        compiler_params=_cparams(("parallel", "parallel", "arbitrary")), name="paged_block_means",
    )(page_table.reshape(-1), *([cache_k] * PAGES_PER_STEP))


def _gate_topk_kernel(q_ref, km_ref, o_ref):
    km = km_ref[0, 0]
    nb = km.shape[0]
    gate = jnp.sum(km * q_ref[...], axis=-1, keepdims=True)
    blk_id = lax.broadcasted_iota(jnp.int32, gate.shape, 0)
    for r in range(MOBA_TOPK):
        best = jnp.max(gate, axis=0, keepdims=True)
        idx = jnp.min(jnp.where(gate == best, blk_id, nb), axis=0, keepdims=True)
        o_ref[0, r] = jnp.broadcast_to(idx[0], o_ref.shape[2:])
        gate = jnp.where(blk_id == idx, -jnp.inf, gate)


def _gate_topk(q3, kmean_all, layer):
    s, n_heads, hd = q3.shape
    nb = kmean_all.shape[2]
    assert nb >= MOBA_TOPK
    out = pl.pallas_call(
        _gate_topk_kernel, grid=(s,),
        in_specs=[pl.BlockSpec((1, n_heads, hd), lambda b: (b, 0, 0)),
                  pl.BlockSpec((1, 1, nb, n_heads, hd), lambda b: (layer, b, 0, 0, 0))],
        out_specs=pl.BlockSpec((1, MOBA_TOPK, n_heads, LANES), lambda b: (b, 0, 0, 0)),
        out_shape=jax.ShapeDtypeStruct((s, MOBA_TOPK, n_heads, LANES), jnp.int32),
        compiler_params=_cparams(("parallel",)), name="decode_gate_topk",
    )(q3, kmean_all)
    return out[..., 0]


def _moba_decode_kernel(pt_ref, pick_ref, q_ref, kn_ref, vn_ref, *refs, n_pages_sel):
    del pt_ref, pick_ref
    o_ref = refs[-1]
    kp, vp = refs[:n_pages_sel], refs[n_pages_sel:2 * n_pages_sel]
    h = pl.program_id(1)
    q = q_ref[0]
    s_own = jnp.sum(kn_ref[0] * q, axis=-1, keepdims=True)
    ss = [jnp.sum(kr[0, 0] * q, axis=-1, keepdims=True) for kr in kp]
    m = s_own
    for s in ss:
        m = jnp.maximum(m, jnp.max(s, axis=0))
    p_own = jnp.exp(s_own - m)
    den = p_own
    acc = p_own * vn_ref[0]
    for s, vr in zip(ss, vp):
        p = jnp.exp(s - m)
        den = den + jnp.sum(p, axis=0)
        acc = acc + jnp.sum(p * vr[0, 0], axis=0)
    out = acc / den

    @pl.when(h == 0)
    def _():
        o_ref[0] = jnp.zeros(o_ref.shape[1:], o_ref.dtype)

    rows = lax.broadcasted_iota(jnp.int32, out.shape, 0)
    o_ref[0] = jnp.where(rows == h, out, o_ref[0])


def _moba_decode(q3, k_new3, v_new3, cache_k, cache_v, page_table, picks, layer):
    s, n_heads, hd = q3.shape
    page = cache_k.shape[2]
    n_pages = page_table.shape[1]
    ppb = MOBA_BLOCK // page
    n_sel = MOBA_TOPK * ppb

    def page_spec(r, t):
        def imap(b, h, pt, pk):
            blk = pk[(b * MOBA_TOPK + r) * n_heads + h]
            return (layer, pt[b * n_pages + blk * ppb + t], 0, 0, 0)
        return pl.BlockSpec((1, 1, page, n_heads, hd), imap)

    vec = pl.BlockSpec((1, n_heads, hd), lambda b, h, pt, pk: (b, 0, 0))
    page_specs = [page_spec(r, t) for r in range(MOBA_TOPK) for t in range(ppb)]
    grid_spec = pltpu.PrefetchScalarGridSpec(
        num_scalar_prefetch=2, grid=(s, n_heads), in_specs=[vec, vec, vec] + page_specs + page_specs,
        out_specs=vec)
    return pl.pallas_call(
        functools.partial(_moba_decode_kernel, n_pages_sel=n_sel), grid_spec=grid_spec,
        out_shape=jax.ShapeDtypeStruct((s, n_heads, hd), F32),
        compiler_params=_cparams(("parallel", "arbitrary")), name="moba_decode",
    )(page_table.reshape(-1), picks.reshape(-1), q3, k_new3, v_new3,
      *([cache_k] * n_sel), *([cache_v] * n_sel))


PROJ_TILE = 256
CONV_TILE = 256
SSD_TILE = 256
MERGE_TILE = 512
FFN_TILE = 512


def _prep_layer_weights(lw, dims):
    d_model, cw, aw, sw, xw, n_ssm_heads = dims
    w_in = lw['w_in']
    main = 2 * cw + 3 * aw + sw + xw
    n_heads = aw // HEAD_DIM
    hidden = lw['w_ffn_out'].shape[0]
    proj = {
        'g1': lw['norm1_g'].reshape(1, d_model),
        'wm': w_in[:, :main].astype(BF16),
        'wdt': jnp.pad(w_in[:, main:main + n_ssm_heads], ((0, 0), (0, LANES - n_ssm_heads))).astype(BF16),
        'wg': w_in[:, main + n_ssm_heads:].astype(BF16),
        'qg': jnp.tile(lw['q_norm_g'], n_heads).reshape(1, aw),
        'kg': jnp.tile(lw['k_norm_g'], n_heads).reshape(1, aw),
        'pm': _head_mean_matrix(aw),
    }
    merge = {'wc': lw['w_branch_conv'].astype(BF16), 'wa': lw['w_branch_att'].astype(BF16),
             'ws': lw['w_branch_ssm'].astype(BF16), 'wo': lw['w_out'].astype(BF16)}
    ffn = {'g2': lw['norm2_g'].reshape(1, d_model), 'wa': lw['w_ffn_in'][:, :hidden].astype(BF16),
           'wg': lw['w_ffn_in'][:, hidden:].astype(BF16), 'wo': lw['w_ffn_out'].astype(BF16)}
    return proj, merge, ffn


def kernel(x_prompt, x_sample, cache_k, cache_v, state_conv, state_ssm_conv, state_ssm, page_table,
           norm1_g, w_in, conv_w, conv_b, conv_ln_g, conv_ln_b, q_norm_g, k_norm_g, ssm_conv_w, ssm_conv_b,
           dt_bias, a_log, d_skip, ssm_norm_g, w_branch_conv, w_branch_att, w_branch_ssm, w_out, norm2_g,
           w_ffn_in, w_ffn_out):
    bp, lp, d_model = x_prompt.shape
    bs, ls, _ = x_sample.shape
    depth, n_pool, page, att_heads, head_dim = cache_k.shape
    assert head_dim == HEAD_DIM and ls == 1
    cw = conv_w.shape[-1]
    aw = att_heads * HEAD_DIM
    sw = ssm_norm_g.shape[-1]
    xw = ssm_conv_w.shape[-1]
    n_ssm_heads = a_log.shape[-1]
    assert xw == sw + 2 * SSM_GROUPS * SSM_STATE and sw == n_ssm_heads * SSM_HEAD_DIM
    assert conv_w.shape[1] == CONV_WIDTH and ssm_conv_w.shape[1] == SSM_CONV
    dims = (d_model, cw, aw, sw, xw, n_ssm_heads)
    past_len = page_table.shape[1] * page
    assert past_len % MOBA_BLOCK == 0 and lp % PROJ_TILE == 0

    cos_p, sin_p = _rope_tables(jnp.arange(lp, dtype=jnp.int32), att_heads)
    cos_s, sin_s = _rope_tables(jnp.full((bs,), past_len, jnp.int32), att_heads)
    kmean_past = _paged_block_means(cache_k, page_table)

    hp = x_prompt.reshape(bp * lp, d_model)
    hs = x_sample.reshape(bs, d_model)
    outs = {name: [] for name in ('kp', 'vp', 'cp', 'scp', 'sp', 'ks', 'vs', 'cs', 'scs', 'ss')}
    for l in range(depth):
        lw = {'w_in': w_in[l], 'norm1_g': norm1_g[l], 'q_norm_g': q_norm_g[l], 'k_norm_g': k_norm_g[l],
              'conv_w': conv_w[l], 'conv_b': conv_b[l], 'conv_ln_g': conv_ln_g[l], 'conv_ln_b': conv_ln_b[l],
              'ssm_conv_w': ssm_conv_w[l], 'ssm_conv_b': ssm_conv_b[l], 'dt_bias': dt_bias[l],
              'a_log': a_log[l], 'd_skip': d_skip[l], 'ssm_norm_g': ssm_norm_g[l],
              'w_branch_conv': w_branch_conv[l], 'w_branch_att': w_branch_att[l],
              'w_branch_ssm': w_branch_ssm[l], 'w_out': w_out[l], 'norm2_g': norm2_g[l],
              'w_ffn_in': w_ffn_in[l], 'w_ffn_out': w_ffn_out[l]}
        pw, mw, fw = _prep_layer_weights(lw, dims)

        u, q, k, v, z, xbc, dt_raw, gates, k_b, v_b, kmean = _project(
            hp, pw, cos_p, sin_p, dims, PROJ_TILE, lp // PROJ_TILE, block_stats=True)
        seq = lambda a: a.reshape(bp, lp, a.shape[-1])
        o_conv = _conv_branch_seq(seq(u), lw, CONV_TILE)
        o_att = _moba_seq(seq(q), seq(k_b), seq(v_b), kmean.reshape(bp, lp // MOBA_BLOCK, aw))
        o_ssm, h_fin = _ssd_seq(seq(xbc), seq(dt_raw), seq(z), lw, SSD_TILE)
        flat = lambda a: a.reshape(bp * lp, a.shape[-1])
        hp = _merge(hp, flat(o_conv), flat(o_att), flat(o_ssm), gates, mw, MERGE_TILE)
        hp = _ffn(hp, fw, FFN_TILE)
        outs['kp'].append(k.reshape(bp, lp, att_heads, HEAD_DIM))
        outs['vp'].append(v.reshape(bp, lp, att_heads, HEAD_DIM))
        outs['cp'].append(seq(u)[:, lp - (CONV_WIDTH - 1):])
        outs['scp'].append(seq(xbc)[:, lp - (SSM_CONV - 1):])
        outs['sp'].append(h_fin.reshape(bp, n_ssm_heads, SSM_HEAD_DIM, SSM_STATE))

        u, q, k, v, z, xbc, dt_raw, gates = _project(hs, pw, cos_s, sin_s, dims, bs, 1, block_stats=False)
        o_conv, xs, bm, cm, xdt, dec = _decode_mix(u, state_conv[l], xbc, state_ssm_conv[l], dt_raw, lw, sw)
        h_new, o_ssm = _decode_state(state_ssm[l].reshape(bs, sw, SSM_STATE), xdt, dec, xs, z, bm, cm, lw, sw)
        heads = lambda a: a.reshape(bs, att_heads, HEAD_DIM)
        picks = _gate_topk(heads(q), kmean_past, l)
        o_att = _moba_decode(heads(q), heads(k), heads(v), cache_k, cache_v, page_table, picks, l)
        hs = _merge(hs, o_conv, o_att.reshape(bs, aw).astype(BF16), o_ssm.reshape(bs, sw), gates, mw, bs)
        hs = _ffn(hs, fw, bs)
        outs['ks'].append(k.reshape(bs, 1, att_heads, HEAD_DIM))
        outs['vs'].append(v.reshape(bs, 1, att_heads, HEAD_DIM))
        outs['cs'].append(jnp.concatenate([state_conv[l][:, 1:], u[:, None, :]], axis=1))
        outs['scs'].append(jnp.concatenate([state_ssm_conv[l][:, 1:], xbc[:, None, :]], axis=1))
        outs['ss'].append(h_new.reshape(bs, n_ssm_heads, SSM_HEAD_DIM, SSM_STATE))

    st = lambda name: jnp.stack(outs[name])
    return (hp.reshape(bp, lp, d_model), hs.reshape(bs, 1, d_model),
            st('kp'), st('vp'), st('cp'), st('scp'), st('sp'),
            st('ks'), st('vs'), st('cs'), st('scs'), st('ss'))
```

```python
import functools

import numpy as np
import jax
import jax.numpy as jnp
from jax import lax
from jax.experimental import pallas as pl
from jax.experimental.pallas import tpu as pltpu

F32 = jnp.float32
BF16 = jnp.bfloat16

CONV_WIDTH = 31
HEAD_DIM = 64
MOBA_BLOCK = 256
MOBA_TOPK = 3
ROPE_THETA = 10000.0
SSM_HEAD_DIM = 64
SSM_GROUPS = 2
SSM_STATE = 128
SSM_CONV = 4
SSD_CHUNK = 128
NORM_EPS = 1e-6

LANES = 128
SUBLANES = 8
VMEM_LIMIT_BYTES = 56 * 1024 * 1024

NEG_BIG = -1e30
LOG2_E = 1.4426950408889634
HEADS_PER_LANE_TILE = LANES // HEAD_DIM


def _cparams(sem=None):
    return pltpu.CompilerParams(dimension_semantics=sem, vmem_limit_bytes=VMEM_LIMIT_BYTES)


def _resident(shape):
    nd = len(shape)
    return pl.BlockSpec(shape, lambda *_: (0,) * nd, pipeline_mode=pl.Buffered(1))


def _split3(x):
    hi = x.astype(BF16)
    r1 = x - hi.astype(F32)
    mid = r1.astype(BF16)
    lo = (r1 - mid.astype(F32)).astype(BF16)
    return hi, mid, lo


def _dot_nt(a, b):
    return lax.dot_general(a, b, (((1,), (1,)), ((), ())), preferred_element_type=F32)


def _dot_f32_by_exact(a, b_bf16, nt=False):
    f = (lambda p: _dot_nt(p, b_bf16)) if nt else (lambda p: jnp.dot(p, b_bf16, preferred_element_type=F32))
    hi, mid, lo = _split3(a)
    return f(hi) + f(mid) + f(lo)


def _sigmoid(x):
    return 1.0 / (1.0 + jnp.exp(-x))


def _silu(x):
    return x * _sigmoid(x)


def _softplus(x):
    return jnp.maximum(x, 0.0) + jnp.log(1.0 + jnp.exp(-jnp.abs(x)))


def _rmsnorm_rows(x, gain):
    return (x * lax.rsqrt(jnp.mean(x * x, axis=-1, keepdims=True) + NORM_EPS)) * gain


def _proj_kernel(x_ref, g1_ref, wm_ref, wdt_ref, wg_ref, cos_ref, sin_ref, qg_ref, kg_ref, pm_ref,
                 u_ref, q_ref, k_ref, v_ref, z_ref, xbc_ref, dt_ref, gate_ref, *block_refs,
                 widths, blocks_per_tile, q_scale):
    cw, aw, sw, xw = widths
    x = x_ref[...]
    hn = _rmsnorm_rows(x, g1_ref[...]).astype(BF16)

    def proj(lo, width):
        return jnp.dot(hn, wm_ref[:, lo:lo + width], preferred_element_type=F32)

    off = 0
    a = proj(off, cw)
    g = proj(off + cw, cw)
    u_ref[...] = a * _sigmoid(g)
    off += 2 * cw

    lane = lax.broadcasted_iota(jnp.int32, (x.shape[0], aw), 1)
    first_half = (lane % HEAD_DIM) < (HEAD_DIM // 2)
    cos = cos_ref[...]
    sin = sin_ref[...]

    def qk_norm_rope(t, gain_ref):
        ss = jnp.dot((t * t).astype(BF16), pm_ref[...], preferred_element_type=F32)
        tn = (t * lax.rsqrt(ss + NORM_EPS)) * gain_ref[...]
        up = pltpu.roll(tn, aw - HEAD_DIM // 2, axis=1)
        dn = pltpu.roll(tn, HEAD_DIM // 2, axis=1)
        return tn * cos + jnp.where(first_half, up, dn) * sin

    q = qk_norm_rope(proj(off, aw), qg_ref)
    q_ref[...] = (q * q_scale).astype(q_ref.dtype)
    off += aw
    k = qk_norm_rope(proj(off, aw), kg_ref)
    off += aw
    v = proj(off, aw)
    off += aw
    if blocks_per_tile:
        k_ref[0] = k.T
        v_ref[0] = v.T
    else:
        k_ref[...] = k
        v_ref[...] = v
    z_ref[...] = proj(off, sw)
    off += sw
    xbc_ref[...] = proj(off, xw)
    dt_ref[...] = jnp.dot(hn, wdt_ref[...], preferred_element_type=F32)
    gate_ref[...] = _sigmoid(jnp.dot(hn, wg_ref[...], preferred_element_type=F32)).astype(BF16)
    if blocks_per_tile:
        kb_ref, vb_ref, km_ref = block_refs
        kb_ref[...] = k.astype(BF16)
        vb_ref[...] = v.astype(BF16)
        for c in range(blocks_per_tile):
            km_ref[0, c:c + 1, :] = jnp.mean(k[c * MOBA_BLOCK:(c + 1) * MOBA_BLOCK], axis=0, keepdims=True)


def _rope_tables(pos, n_heads):
    half = HEAD_DIM // 2
    inv = jnp.power(ROPE_THETA, -jnp.arange(half, dtype=F32) * (2.0 / HEAD_DIM))
    ang = pos.astype(F32)[:, None] * inv[None, :]
    cos = jnp.cos(ang)
    sin = jnp.sin(ang)
    cos_h = jnp.concatenate([cos, cos], axis=-1)
    sin_h = jnp.concatenate([-sin, sin], axis=-1)
    return jnp.tile(cos_h, (1, n_heads)), jnp.tile(sin_h, (1, n_heads))


def _head_mean_matrix(width):
    idx = np.arange(width) // HEAD_DIM
    return jnp.asarray((idx[:, None] == idx[None, :]).astype(np.float32) / HEAD_DIM, dtype=BF16)


def _project(x2d, pw, cos, sin, dims, tile_rows, seq_tiles, block_stats):
    d_model, cw, aw, sw, xw, _ = dims
    n = x2d.shape[0]
    tm = tile_rows
    assert n % tm == 0
    bpt = tm // MOBA_BLOCK if block_stats else 0
    assert not block_stats or tm % MOBA_BLOCK == 0
    n_gate = pw['wg'].shape[1]
    row = lambda w: pl.BlockSpec((tm, w), lambda i: (i, 0))
    tab = pl.BlockSpec((tm, aw), lambda i: (i % seq_tiles, 0))
    in_specs = [row(d_model), _resident((1, d_model)), _resident(pw['wm'].shape), _resident(pw['wdt'].shape),
                _resident(pw['wg'].shape), tab, tab, _resident((1, aw)), _resident((1, aw)), _resident((aw, aw))]
    sds = jax.ShapeDtypeStruct
    out_shape = [sds((n, cw), F32), sds((n, aw), BF16 if block_stats else F32), sds((n, aw), F32),
                 sds((n, aw), F32), sds((n, sw), F32), sds((n, xw), F32), sds((n, LANES), F32),
                 sds((n, n_gate), BF16)]
    out_specs = [row(cw), row(aw), row(aw), row(aw), row(sw), row(xw), row(LANES), row(n_gate)]
    if block_stats:
        seq_len = seq_tiles * tm
        kv_t = pl.BlockSpec((1, aw, tm), lambda i: (i // seq_tiles, 0, i % seq_tiles))
        out_shape[2] = out_shape[3] = sds((n // seq_len, aw, seq_len), F32)
        out_specs[2] = out_specs[3] = kv_t
        out_shape += [sds((n, aw), BF16), sds((n, aw), BF16), sds((n // tm, bpt, aw), F32)]
        out_specs += [row(aw), row(aw), pl.BlockSpec((1, bpt, aw), lambda i: (i, 0, 0))]
    q_scale = HEAD_DIM ** -0.5 * (LOG2_E if block_stats else 1.0)
    return pl.pallas_call(
        functools.partial(_proj_kernel, widths=(cw, aw, sw, xw), blocks_per_tile=bpt, q_scale=q_scale),
        grid=(n // tm,), in_specs=in_specs, out_specs=out_specs, out_shape=out_shape,
        compiler_params=_cparams(("parallel",)), name="in_proj",
    )(x2d, pw['g1'], pw['wm'], pw['wdt'], pw['wg'], cos, sin, pw['qg'], pw['kg'], pw['pm'])


CONV_HALO = 32
CONV_ROWS = 32


def _conv_seq_kernel(u_ref, w_ref, b_ref, lg_ref, lb_ref, o_ref, ext_ref, sh_ref, *, tile):
    @pl.when(pl.program_id(1) == 0)
    def _():
        ext_ref[0:CONV_HALO, :] = jnp.zeros((CONV_HALO, ext_ref.shape[1]), F32)

    ext_ref[CONV_HALO:CONV_HALO + tile, :] = u_ref[0]
    rows = CONV_HALO + tile
    ext = ext_ref[...]
    sh_ref[0] = ext
    for b in range(1, SUBLANES):
        sh_ref[b] = pltpu.roll(ext, rows - b, axis=0)
    base = CONV_HALO - (CONV_WIDTH - 1)
    for r0 in range(0, tile, CONV_ROWS):
        acc = None
        for kk in range(CONV_WIDTH):
            j = base + kk
            lo = r0 + (j // SUBLANES) * SUBLANES
            term = sh_ref[j % SUBLANES, lo:lo + CONV_ROWS, :] * w_ref[kk:kk + 1, :]
            acc = term if acc is None else acc + term
        y = acc + b_ref[...]
        mu = jnp.mean(y, axis=-1, keepdims=True)
        yc = y - mu
        var = jnp.mean(yc * yc, axis=-1, keepdims=True)
        yn = (yc * lax.rsqrt(var + NORM_EPS)) * lg_ref[...] + lb_ref[...]
        o_ref[0, r0:r0 + CONV_ROWS, :] = _silu(yn).astype(o_ref.dtype)
    ext_ref[0:CONV_HALO, :] = ext_ref[tile:tile + CONV_HALO, :]


def _conv_branch_seq(u, lw, tile):
    bsz, L, c = u.shape
    assert L % tile == 0 and tile >= CONV_HALO
    vec = lambda a: a.reshape(1, c)
    return pl.pallas_call(
        functools.partial(_conv_seq_kernel, tile=tile),
        grid=(bsz, L // tile),
        in_specs=[pl.BlockSpec((1, tile, c), lambda b, l: (b, l, 0)), _resident((CONV_WIDTH, c)),
                  _resident((1, c)), _resident((1, c)), _resident((1, c))],
        out_specs=pl.BlockSpec((1, tile, c), lambda b, l: (b, l, 0)),
        out_shape=jax.ShapeDtypeStruct((bsz, L, c), BF16),
        scratch_shapes=[pltpu.VMEM((CONV_HALO + tile, c), F32), pltpu.VMEM((SUBLANES, CONV_HALO + tile, c), F32)],
        compiler_params=_cparams(("parallel", "arbitrary")), name="conv_branch",
    )(u, lw['conv_w'], vec(lw['conv_b']), vec(lw['conv_ln_g']), vec(lw['conv_ln_b']))


def _moba_seq_kernel(q_ref, k_ref, v_ref, km_ref, blockmask_ref, o_ref, sa_ref, sb_ref, *, n_blocks,
                     chunk_blocks):
    i = pl.program_id(2)
    blk = MOBA_BLOCK
    ck = chunk_blocks * blk
    nh = HEADS_PER_LANE_TILE
    q_pair = q_ref[0]
    lane = lax.broadcasted_iota(jnp.int32, (blk, LANES), 1)
    blk_id = lax.broadcasted_iota(jnp.int32, (n_blocks, blk), 0)
    km_parts = _split3(km_ref[0])

    q_aug = []
    for h in range(nh):
        qh = jnp.where((lane // HEAD_DIM) == h, q_pair, jnp.zeros_like(q_pair))
        gate = _dot_nt(km_parts[0], qh) + _dot_nt(km_parts[1], qh) + _dot_nt(km_parts[2], qh)
        gate = jnp.where(blk_id < i, gate, -jnp.inf)
        unpicked = jnp.ones((n_blocks, blk), F32)
        for _ in range(MOBA_TOPK):
            best = jnp.max(gate, axis=0, keepdims=True)
            idx = jnp.min(jnp.where(gate == best, blk_id, n_blocks), axis=0, keepdims=True)
            hit = blk_id == idx
            unpicked = jnp.where(hit, 0.0, unpicked)
            gate = jnp.where(hit, -jnp.inf, gate)
        unpicked = jnp.concatenate([unpicked, jnp.zeros((LANES - n_blocks, blk), F32)], axis=0)
        q_aug.append(jnp.concatenate([qh, unpicked.T.astype(BF16)], axis=1))

    qa = jnp.concatenate(q_aug, axis=0)
    own = pl.multiple_of(i * blk, blk)
    row = lax.broadcasted_iota(jnp.int32, (nh * blk, blk), 0) % blk
    col = lax.broadcasted_iota(jnp.int32, (nh * blk, blk), 1)
    s = jnp.where(col <= row, _dot_nt(qa[:, 0:LANES], k_ref[0, pl.ds(own, blk), :]), NEG_BIG)
    m = jnp.max(s, axis=-1, keepdims=True)
    p = jnp.exp2(s - m)
    l = jnp.sum(p, axis=-1, keepdims=True)
    acc = jnp.dot(p.astype(BF16), v_ref[0, pl.ds(own, blk), :], preferred_element_type=F32)

    last_chunk = n_blocks // chunk_blocks - 1

    def score_into(dst_ref, c):
        start = pl.multiple_of(jnp.minimum(c, last_chunk) * ck, ck)
        k_aug = jnp.concatenate([k_ref[0, pl.ds(start, ck), :], blockmask_ref[pl.ds(start, ck), :]], axis=1)
        dst_ref[...] = _dot_nt(qa, k_aug)

    def absorb(src_ref, c, m, l, acc):
        s = src_ref[...]
        m_new = jnp.maximum(m, jnp.max(s, axis=-1, keepdims=True))
        alpha = jnp.exp2(m - m_new)
        p = jnp.exp2(s - m_new)
        l = l * alpha + jnp.sum(p, axis=-1, keepdims=True)
        v_c = v_ref[0, pl.ds(pl.multiple_of(c * ck, ck), ck), :]
        return m_new, l, acc * alpha + jnp.dot(p.astype(BF16), v_c, preferred_element_type=F32)

    def body(c2, carry):
        m, l, acc = carry
        c = 2 * c2
        score_into(sb_ref, c + 1)
        m, l, acc = absorb(sa_ref, c, m, l, acc)
        score_into(sa_ref, c + 2)
        return absorb(sb_ref, c + 1, m, l, acc)

    n_chunks = (i + chunk_blocks - 1) // chunk_blocks
    score_into(sa_ref, 0)
    m, l, acc = lax.fori_loop(0, (n_chunks + 1) // 2, body, (m, l, acc))
    out = acc / l
    o_ref[0] = jnp.where(lane < HEAD_DIM, out[0:blk], out[blk:2 * blk]).astype(o_ref.dtype)


MOBA_CHUNK_BLOCKS = 2


def _moba_seq(q, k, v, kmean):
    bsz, L, w = q.shape
    nb = L // MOBA_BLOCK
    assert L % MOBA_BLOCK == 0 and w % LANES == 0 and nb <= LANES and HEADS_PER_LANE_TILE == 2
    cb = MOBA_CHUNK_BLOCKS if nb % (2 * MOBA_CHUNK_BLOCKS) == 0 else 1
    assert (nb // cb) % 2 == 0
    score_buf = pltpu.VMEM((HEADS_PER_LANE_TILE * MOBA_BLOCK, cb * MOBA_BLOCK), F32)
    blockmask = np.zeros((L, LANES), np.float32)
    blockmask[np.arange(L), np.arange(L) // MOBA_BLOCK] = NEG_BIG
    kv_spec = pl.BlockSpec((1, L, LANES), lambda b, hp, i: (b, 0, hp))
    return pl.pallas_call(
        functools.partial(_moba_seq_kernel, n_blocks=nb, chunk_blocks=cb),
        grid=(bsz, w // LANES, nb),
        in_specs=[pl.BlockSpec((1, MOBA_BLOCK, LANES), lambda b, hp, i: (b, i, hp)), kv_spec, kv_spec,
                  pl.BlockSpec((1, nb, LANES), lambda b, hp, i: (b, 0, hp)), _resident((L, LANES))],
        out_specs=pl.BlockSpec((1, MOBA_BLOCK, LANES), lambda b, hp, i: (b, i, hp)),
        out_shape=jax.ShapeDtypeStruct((bsz, L, w), BF16), scratch_shapes=[score_buf, score_buf],
        compiler_params=_cparams(("parallel", "parallel", "arbitrary")), name="moba_prompt",
    )(q, k, v, kmean, jnp.asarray(blockmask, dtype=BF16))


SSM_HALO = SUBLANES


def _ssd_seq_kernel(xbc_ref, dt_ref, z_ref, cw_ref, cb_ref, dtb_ref, a_ref, dsk_ref, ng_ref, tri_ref,
                    o_ref, h_ref, ext_ref, y_ref, *, tile, inner, n_pairs):
    q = SSD_CHUNK
    gs = SSM_GROUPS * SSM_STATE

    @pl.when(pl.program_id(1) == 0)
    def _():
        ext_ref[0:SSM_HALO, :] = jnp.zeros((SSM_HALO, ext_ref.shape[1]), F32)
        h_ref[...] = jnp.zeros_like(h_ref)

    ext_ref[SSM_HALO:SSM_HALO + tile, :] = xbc_ref[0]
    base = SSM_HALO - (SSM_CONV - 1)
    acc = ext_ref[base:base + tile, :] * cw_ref[0:1, :]
    for kk in range(1, SSM_CONV):
        acc = acc + ext_ref[base + kk:base + kk + tile, :] * cw_ref[kk:kk + 1, :]
    xc = _silu(acc + cb_ref[...])
    ext_ref[0:SSM_HALO, :] = ext_ref[tile:tile + SSM_HALO, :]

    dt = _softplus(dt_ref[0] + dtb_ref[...])
    da = dt * a_ref[...]
    tri = tri_ref[...]
    rows = lax.broadcasted_iota(jnp.int32, (q, q), 0)
    cols = lax.broadcasted_iota(jnp.int32, (q, q), 1)
    lower = cols <= rows
    lane = lax.broadcasted_iota(jnp.int32, (q, LANES), 1)
    left = lane < SSM_HEAD_DIM
    top = lax.broadcasted_iota(jnp.int32, (LANES, SSM_STATE), 0) < SSM_HEAD_DIM

    for c in range(tile // q):
        r0 = c * q
        hi, mid, lo = _split3(da[r0:r0 + q])
        cum = (jnp.dot(tri, hi, preferred_element_type=F32) + jnp.dot(tri, mid, preferred_element_type=F32)
               + jnp.dot(tri, lo, preferred_element_type=F32))
        cum_t = cum.T
        e_cum = jnp.exp(cum)
        e_rest = jnp.exp(cum[q - 1:q, :] - cum)
        e_last_t = jnp.exp(cum_t[:, q - 1:q])
        dt_c = dt[r0:r0 + q]
        bm = xc[r0:r0 + q, inner:inner + gs]
        cm = xc[r0:r0 + q, inner + gs:inner + 2 * gs]
        for p in range(n_pairs):
            g = (p * HEADS_PER_LANE_TILE * SSM_HEAD_DIM) // (inner // SSM_GROUPS)
            b_g = bm[:, g * SSM_STATE:(g + 1) * SSM_STATE].astype(BF16)
            c_g = cm[:, g * SSM_STATE:(g + 1) * SSM_STATE].astype(BF16)
            cb = _dot_nt(c_g, b_g)
            h0, h1 = HEADS_PER_LANE_TILE * p, HEADS_PER_LANE_TILE * p + 1
            xs = xc[r0:r0 + q, p * LANES:(p + 1) * LANES]
            xdt = xs * jnp.where(left, dt_c[:, h0:h0 + 1], dt_c[:, h1:h1 + 1])
            xdt_b = xdt.astype(BF16)
            y_pair = None
            for hh, head in enumerate((h0, h1)):
                seg = cum[:, head:head + 1] - cum_t[head:head + 1, :]
                lmat = jnp.where(lower, jnp.exp(seg), 0.0)
                y_h = jnp.dot((cb * lmat).astype(BF16), xdt_b, preferred_element_type=F32)
                y_pair = y_h if hh == 0 else jnp.where(left, y_pair, y_h)
            h_pair = h_ref[0, p * LANES:(p + 1) * LANES, :]
            inter = _dot_nt(c_g, h_pair.astype(BF16))
            y_pair = y_pair + inter * jnp.where(left, e_cum[:, h0:h0 + 1], e_cum[:, h1:h1 + 1])
            y_ref[r0:r0 + q, p * LANES:(p + 1) * LANES] = y_pair
            xdd = xdt * jnp.where(left, e_rest[:, h0:h0 + 1], e_rest[:, h1:h1 + 1])
            upd = jnp.dot(xdd.T.astype(BF16), b_g, preferred_element_type=F32)
            keep = jnp.where(top, e_last_t[h0:h0 + 1, :], e_last_t[h1:h1 + 1, :])
            h_ref[0, p * LANES:(p + 1) * LANES, :] = h_pair * keep + upd

    y = y_ref[...] + dsk_ref[...] * xc[:, 0:inner]
    gated = y * _silu(z_ref[0])
    o_ref[0] = _rmsnorm_rows(gated, ng_ref[...]).astype(o_ref.dtype)


def _ssd_seq(xbc, dt_raw, z, lw, tile):
    bsz, L, xw = xbc.shape
    inner = z.shape[-1]
    n_heads = inner // SSM_HEAD_DIM
    assert L % tile == 0 and tile % SSD_CHUNK == 0 and inner % LANES == 0
    assert (inner // SSM_GROUPS) % LANES == 0
    pad_h = lambda a: jnp.pad(a, (0, LANES - n_heads)).reshape(1, LANES)
    tri = jnp.asarray(np.tril(np.ones((SSD_CHUNK, SSD_CHUNK), np.float32)), dtype=BF16)
    seq = lambda w: pl.BlockSpec((1, tile, w), lambda b, l: (b, l, 0))
    return pl.pallas_call(
        functools.partial(_ssd_seq_kernel, tile=tile, inner=inner, n_pairs=inner // LANES),
        grid=(bsz, L // tile),
        in_specs=[seq(xw), seq(LANES), seq(inner), _resident((SSM_CONV, xw)), _resident((1, xw)),
                  _resident((1, LANES)), _resident((1, LANES)), _resident((1, inner)), _resident((1, inner)),
                  _resident((SSD_CHUNK, SSD_CHUNK))],
        out_specs=[seq(inner), pl.BlockSpec((1, inner, SSM_STATE), lambda b, l: (b, 0, 0))],
        out_shape=[jax.ShapeDtypeStruct((bsz, L, inner), BF16),
                   jax.ShapeDtypeStruct((bsz, inner, SSM_STATE), F32)],
        scratch_shapes=[pltpu.VMEM((SSM_HALO + tile, xw), F32), pltpu.VMEM((tile, inner), F32)],
        compiler_params=_cparams(("parallel", "arbitrary")), name="ssd_scan",
    )(xbc, dt_raw, z, lw['ssm_conv_w'], lw['ssm_conv_b'].reshape(1, xw), pad_h(lw['dt_bias']),
      pad_h(-jnp.exp(lw['a_log'])), jnp.repeat(lw['d_skip'], SSM_HEAD_DIM).reshape(1, inner),
      lw['ssm_norm_g'].reshape(1, inner), tri)


def _merge_kernel(x_ref, oc_ref, oa_ref, os_ref, gate_ref, wc_ref, wa_ref, ws_ref, wo_ref, o_ref):
    d = x_ref.shape[1]
    gates = gate_ref[...]
    branch = lambda o, w: jnp.dot(o[...], w[...], preferred_element_type=F32)
    merged = (gates[:, 0:d].astype(F32) * branch(oc_ref, wc_ref)
              + gates[:, d:2 * d].astype(F32) * branch(oa_ref, wa_ref)
              + gates[:, 2 * d:3 * d].astype(F32) * branch(os_ref, ws_ref))
    o_ref[...] = x_ref[...] + jnp.dot(merged.astype(BF16), wo_ref[...], preferred_element_type=F32)


def _merge(x2d, o_conv, o_att, o_ssm, gates, mw, tile_rows):
    n, d = x2d.shape
    tm = tile_rows
    assert n % tm == 0
    row = lambda w: pl.BlockSpec((tm, w), lambda i: (i, 0))
    bw = o_conv.shape[1]
    return pl.pallas_call(
        _merge_kernel, grid=(n // tm,),
        in_specs=[row(d), row(bw), row(bw), row(bw), row(3 * d), _resident((bw, d)), _resident((bw, d)),
                  _resident((bw, d)), _resident((d, d))],
        out_specs=row(d), out_shape=jax.ShapeDtypeStruct((n, d), F32),
        compiler_params=_cparams(("parallel",)), name="merge_out_proj",
    )(x2d, o_conv, o_att, o_ssm, gates, mw['wc'], mw['wa'], mw['ws'], mw['wo'])


def _ffn_kernel(x_ref, g2_ref, wa_ref, wg_ref, wo_ref, o_ref, *, chunk):
    x = x_ref[...]
    hn = _rmsnorm_rows(x, g2_ref[...]).astype(BF16)
    acc = x
    for c in range(wa_ref.shape[1] // chunk):
        sl = slice(c * chunk, (c + 1) * chunk)
        a = jnp.dot(hn, wa_ref[:, sl], preferred_element_type=F32)
        g = jnp.dot(hn, wg_ref[:, sl], preferred_element_type=F32)
        acc = acc + jnp.dot((_silu(a) * g).astype(BF16), wo_ref[sl, :], preferred_element_type=F32)
    o_ref[...] = acc


def _ffn_chunk(hidden):
    best = LANES
    for c in range(LANES, hidden + 1, LANES):
        if hidden % c == 0 and c <= 1536:
            best = c
    return best


def _ffn(x2d, fw, tile_rows):
    n, d = x2d.shape
    tm = tile_rows
    hidden = fw['wa'].shape[1]
    assert n % tm == 0 and hidden % LANES == 0
    row = pl.BlockSpec((tm, d), lambda i: (i, 0))
    return pl.pallas_call(
        functools.partial(_ffn_kernel, chunk=_ffn_chunk(hidden)), grid=(n // tm,),
        in_specs=[row, _resident((1, d)), _resident((d, hidden)), _resident((d, hidden)), _resident((hidden, d))],
        out_specs=row, out_shape=jax.ShapeDtypeStruct((n, d), F32),
        compiler_params=_cparams(("parallel",)), name="swiglu_ffn",
    )(x2d, fw['g2'], fw['wa'], fw['wg'], fw['wo'])


def _decode_mix_kernel(u_ref, cst_ref, cw_ref, cb_ref, lg_ref, lb_ref, xbc_ref, sst_ref, sw_ref, sb_ref,
                       dt_ref, dtb_ref, a_ref, ex_ref,
                       oc_ref, xs_ref, bm_ref, cm_ref, xdt_ref, dec_ref, *, inner):
    gs = SSM_GROUPS * SSM_STATE
    acc = u_ref[...] * cw_ref[CONV_WIDTH - 1:CONV_WIDTH, :] + cb_ref[...]
    for kk in range(CONV_WIDTH - 1):
        acc = acc + cst_ref[kk] * cw_ref[kk:kk + 1, :]
    mu = jnp.mean(acc, axis=-1, keepdims=True)
    yc = acc - mu
    var = jnp.mean(yc * yc, axis=-1, keepdims=True)
    oc_ref[...] = _silu((yc * lax.rsqrt(var + NORM_EPS)) * lg_ref[...] + lb_ref[...]).astype(oc_ref.dtype)

    acc = xbc_ref[...] * sw_ref[SSM_CONV - 1:SSM_CONV, :] + sb_ref[...]
    for kk in range(SSM_CONV - 1):
        acc = acc + sst_ref[kk] * sw_ref[kk:kk + 1, :]
    xc = _silu(acc)
    xs = xc[:, 0:inner]
    xs_ref[...] = xs
    bm_ref[...] = xc[:, inner:inner + gs]
    cm_ref[...] = xc[:, inner + gs:inner + 2 * gs]
    dt = _softplus(dt_ref[...] + dtb_ref[...])
    dec = jnp.exp(dt * a_ref[...])
    xdt_ref[...] = xs * _dot_f32_by_exact(dt, ex_ref[...])
    dec_ref[...] = _dot_f32_by_exact(dec, ex_ref[...])


def _decode_mix(u, conv_state, xbc, ssm_conv_state, dt_raw, lw, inner):
    s, cw = u.shape
    xw = xbc.shape[1]
    n_heads = inner // SSM_HEAD_DIM
    gs = SSM_GROUPS * SSM_STATE
    pad_h = lambda a: jnp.pad(a, (0, LANES - n_heads)).reshape(1, LANES)
    expand = np.zeros((LANES, inner), np.float32)
    for h in range(n_heads):
        expand[h, h * SSM_HEAD_DIM:(h + 1) * SSM_HEAD_DIM] = 1.0
    sds = jax.ShapeDtypeStruct
    return pl.pallas_call(
        functools.partial(_decode_mix_kernel, inner=inner),
        out_shape=[sds((s, cw), BF16), sds((s, inner), F32), sds((s, gs), F32), sds((s, gs), F32),
                   sds((s, inner), F32), sds((s, inner), F32)],
        compiler_params=_cparams(), name="decode_conv_ssm_inputs",
    )(u, jnp.swapaxes(conv_state, 0, 1), lw['conv_w'], lw['conv_b'].reshape(1, cw),
      lw['conv_ln_g'].reshape(1, cw), lw['conv_ln_b'].reshape(1, cw),
      xbc, jnp.swapaxes(ssm_conv_state, 0, 1), lw['ssm_conv_w'], lw['ssm_conv_b'].reshape(1, xw),
      dt_raw, pad_h(lw['dt_bias']), pad_h(-jnp.exp(lw['a_log'])), jnp.asarray(expand, dtype=BF16))


def _decode_state_kernel(h_ref, xcol_ref, dcol_ref, xdt_ref, dec_ref, xs_ref, z_ref, bm_ref, cm_ref,
                         dsk_ref, ng_ref, hn_ref, o_ref, *, inner):
    half = inner // SSM_GROUPS
    h = h_ref[0]
    row = lax.broadcasted_iota(jnp.int32, h.shape, 0)
    b0 = bm_ref[0][:, 0:SSM_STATE]
    b1 = bm_ref[0][:, SSM_STATE:2 * SSM_STATE]
    c0 = cm_ref[0][:, 0:SSM_STATE]
    c1 = cm_ref[0][:, SSM_STATE:2 * SSM_STATE]
    hn_ref[0] = h * dcol_ref[0] + xcol_ref[0] * jnp.where(row < half, b0, b1)

    def c_dot_h(c_row, h_rows):
        ch, cm_, _ = _split3(jnp.broadcast_to(c_row, (SUBLANES, SSM_STATE)))
        hh, hm, _ = _split3(h_rows)
        return (_dot_nt(ch, hh) + _dot_nt(ch, hm) + _dot_nt(cm_, hh))[0:1, :]

    ch = jnp.concatenate([c_dot_h(c0, h[0:half]), c_dot_h(c1, h[half:inner])], axis=1)
    lane = lax.broadcasted_iota(jnp.int32, (1, inner), 1)
    cb = jnp.where(lane < half, jnp.sum(c0 * b0, axis=-1, keepdims=True), jnp.sum(c1 * b1, axis=-1, keepdims=True))
    y = ch * dec_ref[0] + cb * xdt_ref[0] + dsk_ref[...] * xs_ref[0]
    gated = y * _silu(z_ref[0])
    o_ref[0] = _rmsnorm_rows(gated, ng_ref[...]).astype(o_ref.dtype)


def _decode_state(h0, xdt, dec, xs, z, bm, cm, lw, inner):
    s = h0.shape[0]
    gs = SSM_GROUPS * SSM_STATE
    per = lambda shape: pl.BlockSpec((1,) + shape, lambda b: (b, 0, 0))
    r3 = lambda a: a.reshape(s, 1, a.shape[-1])
    return pl.pallas_call(
        functools.partial(_decode_state_kernel, inner=inner), grid=(s,),
        in_specs=[per((inner, SSM_STATE)), per((inner, 1)), per((inner, 1)), per((1, inner)), per((1, inner)),
                  per((1, inner)), per((1, inner)), per((1, gs)), per((1, gs)), _resident((1, inner)),
                  _resident((1, inner))],
        out_specs=[per((inner, SSM_STATE)), per((1, inner))],
        out_shape=[jax.ShapeDtypeStruct((s, inner, SSM_STATE), F32), jax.ShapeDtypeStruct((s, 1, inner), BF16)],
        compiler_params=_cparams(("parallel",)), name="decode_ssm_state",
    )(h0, xdt.reshape(s, inner, 1), dec.reshape(s, inner, 1), r3(xdt), r3(dec), r3(xs), r3(z), r3(bm), r3(cm),
      jnp.repeat(lw['d_skip'], SSM_HEAD_DIM).reshape(1, inner), lw['ssm_norm_g'].reshape(1, inner))


PAGES_PER_STEP = 16


def _dot_f32_pair(a, b, nt=False):
    f = _dot_nt if nt else (lambda x, y: jnp.dot(x, y, preferred_element_type=F32))
    a_hi, a_mid, _ = _split3(a)
    b_hi, b_mid, _ = _split3(b)
    return f(a_hi, b_hi) + f(a_hi, b_mid) + f(a_mid, b_hi)


def _paged_gate_topk_kernel(pt_ref, q_ref, *refs, pages_per_block, n_blocks):
    del pt_ref
    pages, o_ref, ksum_ref = refs[:-2], refs[-2], refs[-1]
    c = pl.program_id(1)
    bps = len(pages) // pages_per_block

    @pl.when(c == 0)
    def _():
        ksum_ref[...] = jnp.zeros_like(ksum_ref)

    width = ksum_ref.shape[0]
    lane = lax.broadcasted_iota(jnp.int32, ksum_ref.shape, 1)
    acc = ksum_ref[...]
    for r in range(bps):
        col = None
        for t in range(pages_per_block):
            tile = pages[r * pages_per_block + t][0, 0].reshape(width, -1)
            part = jnp.sum(tile, axis=-1, keepdims=True)
            col = part if col is None else col + part
        acc = jnp.where(lane == c * bps + r, col, acc)
    ksum_ref[...] = acc

    @pl.when(c == pl.num_programs(1) - 1)
    def _():
        n_heads = o_ref.shape[1]
        head_of_lane = lax.broadcasted_iota(jnp.int32, (n_heads, width), 1) // HEAD_DIM
        head_of_row = lax.broadcasted_iota(jnp.int32, (n_heads, width), 0)
        q_rows = jnp.where(head_of_lane == head_of_row, q_ref[0], 0.0)
        gate = _dot_f32_pair(q_rows, acc)
        blk = lax.broadcasted_iota(jnp.int32, gate.shape, 1)
        gate = jnp.where(blk < n_blocks, gate, -jnp.inf)
        out = jnp.zeros(gate.shape, jnp.int32)
        for r in range(MOBA_TOPK):
            best = jnp.max(gate, axis=-1, keepdims=True)
            idx = jnp.min(jnp.where(gate == best, blk, n_blocks), axis=-1, keepdims=True)
            out = jnp.where(blk == r, idx, out)
            gate = jnp.where(blk == idx, -jnp.inf, gate)
        o_ref[0] = out


def _paged_gate_topk(q, cache_t, page_table, layer):
    s, w = q.shape
    _, _, n_heads, hd, page = cache_t.shape
    n_pages = page_table.shape[1]
    ppb = MOBA_BLOCK // page
    nb = n_pages // ppb
    assert MOBA_BLOCK % page == 0 and n_pages % PAGES_PER_STEP == 0 and PAGES_PER_STEP % ppb == 0
    assert MOBA_TOPK <= nb <= LANES and w == n_heads * hd

    def page_spec(r):
        return pl.BlockSpec((1, 1, n_heads, hd, page),
                            lambda b, c, pt: (layer, pt[b * n_pages + c * PAGES_PER_STEP + r], 0, 0, 0))

    grid_spec = pltpu.PrefetchScalarGridSpec(
        num_scalar_prefetch=1, grid=(s, n_pages // PAGES_PER_STEP),
        in_specs=[pl.BlockSpec((1, 1, w), lambda b, c, pt: (b, 0, 0))] + [page_spec(r) for r in range(PAGES_PER_STEP)],
        out_specs=pl.BlockSpec((1, n_heads, LANES), lambda b, c, pt: (b, 0, 0)),
        scratch_shapes=[pltpu.VMEM((w, LANES), F32)])
    out = pl.pallas_call(
        functools.partial(_paged_gate_topk_kernel, pages_per_block=ppb, n_blocks=nb), grid_spec=grid_spec,
        out_shape=jax.ShapeDtypeStruct((s, n_heads, LANES), jnp.int32),
        compiler_params=_cparams(("parallel", "arbitrary")), name="paged_gate_topk",
    )(page_table.reshape(-1), q.reshape(s, 1, w), *([cache_t] * PAGES_PER_STEP))
    return out[:, :, :MOBA_TOPK]


def _moba_decode_kernel(pt_ref, pick_ref, q_ref, kn_ref, vn_ref, *refs, n_pages_sel):
    del pt_ref, pick_ref
    o_ref = refs[-1]
    kp, vp = refs[:n_pages_sel], refs[n_pages_sel:2 * n_pages_sel]
    q = q_ref[0, 0]
    q_rows = jnp.broadcast_to(q, (SUBLANES, q.shape[1]))
    s_own = jnp.sum(kn_ref[0, 0] * q, axis=-1, keepdims=True)
    ss = [_dot_f32_pair(q_rows, kr[0, 0, 0])[0:1, :] for kr in kp]
    m = s_own
    for s in ss:
        m = jnp.maximum(m, jnp.max(s, axis=-1, keepdims=True))
    p_own = jnp.exp(s_own - m)
    den = p_own
    acc = p_own * vn_ref[0, 0]
    for s, vr in zip(ss, vp):
        p = jnp.exp(s - m)
        den = den + jnp.sum(p, axis=-1, keepdims=True)
        acc = acc + _dot_f32_pair(jnp.broadcast_to(p, (SUBLANES, p.shape[1])), vr[0, 0, 0], nt=True)[0:1, :]
    o_ref[0, 0] = acc / den


def _moba_decode(q, k_new, v_new, cache_kt, cache_vt, page_table, picks, layer):
    s, w = q.shape
    _, _, n_heads, hd, page = cache_kt.shape
    n_pages = page_table.shape[1]
    ppb = MOBA_BLOCK // page
    n_sel = MOBA_TOPK * ppb

    def page_spec(r, t):
        def imap(b, h, pt, pk):
            blk = pk[(b * n_heads + h) * MOBA_TOPK + r]
            return (layer, pt[b * n_pages + blk * ppb + t], h, 0, 0)
        return pl.BlockSpec((1, 1, 1, hd, page), imap)

    vec = pl.BlockSpec((1, 1, 1, hd), lambda b, h, pt, pk: (b, h, 0, 0))
    page_specs = [page_spec(r, t) for r in range(MOBA_TOPK) for t in range(ppb)]
    grid_spec = pltpu.PrefetchScalarGridSpec(
        num_scalar_prefetch=2, grid=(s, n_heads), in_specs=[vec, vec, vec] + page_specs + page_specs,
        out_specs=vec)
    rows = lambda a: a.reshape(s, n_heads, 1, hd)
    out = pl.pallas_call(
        functools.partial(_moba_decode_kernel, n_pages_sel=n_sel), grid_spec=grid_spec,
        out_shape=jax.ShapeDtypeStruct((s, n_heads, 1, hd), F32),
        compiler_params=_cparams(("parallel", "parallel")), name="moba_decode",
    )(page_table.reshape(-1), picks.reshape(-1), rows(q), rows(k_new), rows(v_new),
      *([cache_kt] * n_sel), *([cache_vt] * n_sel))
    return out.reshape(s, w)


PROJ_TILE = 256
CONV_TILE = 256
SSD_TILE = 256
MERGE_TILE = 512
FFN_TILE = 512


def _prep_layer_weights(lw, dims):
    d_model, cw, aw, sw, xw, n_ssm_heads = dims
    w_in = lw['w_in']
    main = 2 * cw + 3 * aw + sw + xw
    n_heads = aw // HEAD_DIM
    hidden = lw['w_ffn_out'].shape[0]
    proj = {
        'g1': lw['norm1_g'].reshape(1, d_model),
        'wm': w_in[:, :main].astype(BF16),
        'wdt': jnp.pad(w_in[:, main:main + n_ssm_heads], ((0, 0), (0, LANES - n_ssm_heads))).astype(BF16),
        'wg': w_in[:, main + n_ssm_heads:].astype(BF16),
        'qg': jnp.tile(lw['q_norm_g'], n_heads).reshape(1, aw),
        'kg': jnp.tile(lw['k_norm_g'], n_heads).reshape(1, aw),
        'pm': _head_mean_matrix(aw),
    }
    merge = {'wc': lw['w_branch_conv'].astype(BF16), 'wa': lw['w_branch_att'].astype(BF16),
             'ws': lw['w_branch_ssm'].astype(BF16), 'wo': lw['w_out'].astype(BF16)}
    ffn = {'g2': lw['norm2_g'].reshape(1, d_model), 'wa': lw['w_ffn_in'][:, :hidden].astype(BF16),
           'wg': lw['w_ffn_in'][:, hidden:].astype(BF16), 'wo': lw['w_ffn_out'].astype(BF16)}
    return proj, merge, ffn


def kernel(x_prompt, x_sample, cache_k, cache_v, state_conv, state_ssm_conv, state_ssm, page_table,
           norm1_g, w_in, conv_w, conv_b, conv_ln_g, conv_ln_b, q_norm_g, k_norm_g, ssm_conv_w, ssm_conv_b,
           dt_bias, a_log, d_skip, ssm_norm_g, w_branch_conv, w_branch_att, w_branch_ssm, w_out, norm2_g,
           w_ffn_in, w_ffn_out):
    bp, lp, d_model = x_prompt.shape
    bs, ls, _ = x_sample.shape
    depth, n_pool, page, att_heads, head_dim = cache_k.shape
    assert head_dim == HEAD_DIM and ls == 1
    cw = conv_w.shape[-1]
    aw = att_heads * HEAD_DIM
    sw = ssm_norm_g.shape[-1]
    xw = ssm_conv_w.shape[-1]
    n_ssm_heads = a_log.shape[-1]
    assert xw == sw + 2 * SSM_GROUPS * SSM_STATE and sw == n_ssm_heads * SSM_HEAD_DIM
    assert conv_w.shape[1] == CONV_WIDTH and ssm_conv_w.shape[1] == SSM_CONV
    dims = (d_model, cw, aw, sw, xw, n_ssm_heads)
    past_len = page_table.shape[1] * page
    assert past_len % MOBA_BLOCK == 0 and lp % PROJ_TILE == 0

    cos_p, sin_p = _rope_tables(jnp.arange(lp, dtype=jnp.int32), att_heads)
    cos_s, sin_s = _rope_tables(jnp.full((bs,), past_len, jnp.int32), att_heads)
    cache_kt = jnp.transpose(cache_k, (0, 1, 3, 4, 2))
    cache_vt = jnp.transpose(cache_v, (0, 1, 3, 4, 2))

    hp = x_prompt.reshape(bp * lp, d_model)
    hs = x_sample.reshape(bs, d_model)
    outs = {name: [] for name in ('kp', 'vp', 'cp', 'scp', 'sp', 'ks', 'vs', 'cs', 'scs', 'ss')}
    for l in range(depth):
        lw = {'w_in': w_in[l], 'norm1_g': norm1_g[l], 'q_norm_g': q_norm_g[l], 'k_norm_g': k_norm_g[l],
              'conv_w': conv_w[l], 'conv_b': conv_b[l], 'conv_ln_g': conv_ln_g[l], 'conv_ln_b': conv_ln_b[l],
              'ssm_conv_w': ssm_conv_w[l], 'ssm_conv_b': ssm_conv_b[l], 'dt_bias': dt_bias[l],
              'a_log': a_log[l], 'd_skip': d_skip[l], 'ssm_norm_g': ssm_norm_g[l],
              'w_branch_conv': w_branch_conv[l], 'w_branch_att': w_branch_att[l],
              'w_branch_ssm': w_branch_ssm[l], 'w_out': w_out[l], 'norm2_g': norm2_g[l],
              'w_ffn_in': w_ffn_in[l], 'w_ffn_out': w_ffn_out[l]}
        pw, mw, fw = _prep_layer_weights(lw, dims)

        u, q, k, v, z, xbc, dt_raw, gates, k_b, v_b, kmean = _project(
            hp, pw, cos_p, sin_p, dims, PROJ_TILE, lp // PROJ_TILE, block_stats=True)
        seq = lambda a: a.reshape(bp, lp, a.shape[-1])
        o_conv = _conv_branch_seq(seq(u), lw, CONV_TILE)
        o_att = _moba_seq(seq(q), seq(k_b), seq(v_b), kmean.reshape(bp, lp // MOBA_BLOCK, aw))
        o_ssm, h_fin = _ssd_seq(seq(xbc), seq(dt_raw), seq(z), lw, SSD_TILE)
        flat = lambda a: a.reshape(bp * lp, a.shape[-1])
        hp = _merge(hp, flat(o_conv), flat(o_att), flat(o_ssm), gates, mw, MERGE_TILE)
        hp = _ffn(hp, fw, FFN_TILE)
        by_pos = lambda a: jnp.transpose(a.reshape(bp, att_heads, HEAD_DIM, lp), (0, 3, 1, 2))
        outs['kp'].append(by_pos(k))
        outs['vp'].append(by_pos(v))
        outs['cp'].append(seq(u)[:, lp - (CONV_WIDTH - 1):])
        outs['scp'].append(seq(xbc)[:, lp - (SSM_CONV - 1):])
        outs['sp'].append(h_fin.reshape(bp, n_ssm_heads, SSM_HEAD_DIM, SSM_STATE))

        u, q, k, v, z, xbc, dt_raw, gates = _project(hs, pw, cos_s, sin_s, dims, bs, 1, block_stats=False)
        o_conv, xs, bm, cm, xdt, dec = _decode_mix(u, state_conv[l], xbc, state_ssm_conv[l], dt_raw, lw, sw)
        h_new, o_ssm = _decode_state(state_ssm[l].reshape(bs, sw, SSM_STATE), xdt, dec, xs, z, bm, cm, lw, sw)
        picks = _paged_gate_topk(q, cache_kt, page_table, l)
        o_att = _moba_decode(q, k, v, cache_kt, cache_vt, page_table, picks, l)
        hs = _merge(hs, o_conv, o_att.astype(BF16), o_ssm.reshape(bs, sw), gates, mw, bs)
        hs = _ffn(hs, fw, bs)
        outs['ks'].append(k.reshape(bs, 1, att_heads, HEAD_DIM))
        outs['vs'].append(v.reshape(bs, 1, att_heads, HEAD_DIM))
        outs['cs'].append(jnp.concatenate([state_conv[l][:, 1:], u[:, None, :]], axis=1))
        outs['scs'].append(jnp.concatenate([state_ssm_conv[l][:, 1:], xbc[:, None, :]], axis=1))
        outs['ss'].append(h_new.reshape(bs, n_ssm_heads, SSM_HEAD_DIM, SSM_STATE))

    st = lambda name: jnp.stack(outs[name])
    return (hp.reshape(bp, lp, d_model), hs.reshape(bs, 1, d_model),
            st('kp'), st('vp'), st('cp'), st('scp'), st('sp'),
            st('ks'), st('vs'), st('cs'), st('scs'), st('ss'))
```

```python
import functools

import numpy as np
import jax
import jax.numpy as jnp
from jax import lax
from jax.experimental import pallas as pl
from jax.experimental.pallas import tpu as pltpu

F32 = jnp.float32
BF16 = jnp.bfloat16

CONV_WIDTH = 31
HEAD_DIM = 64
MOBA_BLOCK = 256
MOBA_TOPK = 3
ROPE_THETA = 10000.0
SSM_HEAD_DIM = 64
SSM_GROUPS = 2
SSM_STATE = 128
SSM_CONV = 4
SSD_CHUNK = 128
NORM_EPS = 1e-6

LANES = 128
SUBLANES = 8
VMEM_LIMIT_BYTES = 56 * 1024 * 1024

NEG_BIG = -1e30
LOG2_E = 1.4426950408889634
HEADS_PER_LANE_TILE = LANES // HEAD_DIM


def _cparams(sem=None):
    return pltpu.CompilerParams(dimension_semantics=sem, vmem_limit_bytes=VMEM_LIMIT_BYTES)


def _resident(shape):
    nd = len(shape)
    return pl.BlockSpec(shape, lambda *_: (0,) * nd, pipeline_mode=pl.Buffered(1))


def _split3(x):
    hi = x.astype(BF16)
    r1 = x - hi.astype(F32)
    mid = r1.astype(BF16)
    lo = (r1 - mid.astype(F32)).astype(BF16)
    return hi, mid, lo


def _dot_nt(a, b):
    return lax.dot_general(a, b, (((1,), (1,)), ((), ())), preferred_element_type=F32)


def _dot_f32_by_exact(a, b_bf16, nt=False):
    f = (lambda p: _dot_nt(p, b_bf16)) if nt else (lambda p: jnp.dot(p, b_bf16, preferred_element_type=F32))
    hi, mid, lo = _split3(a)
    return f(hi) + f(mid) + f(lo)


def _sigmoid(x):
    return 1.0 / (1.0 + jnp.exp(-x))


def _silu(x):
    return x * _sigmoid(x)


def _softplus(x):
    return jnp.maximum(x, 0.0) + jnp.log(1.0 + jnp.exp(-jnp.abs(x)))


def _rmsnorm_rows(x, gain):
    return (x * lax.rsqrt(jnp.mean(x * x, axis=-1, keepdims=True) + NORM_EPS)) * gain


def _proj_kernel(x_ref, g1_ref, wm_ref, wdt_ref, wg_ref, cos_ref, sin_ref, qg_ref, kg_ref, pm_ref,
                 u_ref, q_ref, k_ref, v_ref, z_ref, xbc_ref, dt_ref, gate_ref, *block_refs,
                 widths, blocks_per_tile, q_scale):
    cw, aw, sw, xw = widths
    x = x_ref[...]
    hn = _rmsnorm_rows(x, g1_ref[...]).astype(BF16)

    def proj(lo, width):
        return jnp.dot(hn, wm_ref[:, lo:lo + width], preferred_element_type=F32)

    off = 0
    a = proj(off, cw)
    g = proj(off + cw, cw)
    u_ref[...] = a * _sigmoid(g)
    off += 2 * cw

    lane = lax.broadcasted_iota(jnp.int32, (x.shape[0], aw), 1)
    first_half = (lane % HEAD_DIM) < (HEAD_DIM // 2)
    cos = cos_ref[...]
    sin = sin_ref[...]

    def qk_norm_rope(t, gain_ref):
        ss = jnp.dot((t * t).astype(BF16), pm_ref[...], preferred_element_type=F32)
        tn = (t * lax.rsqrt(ss + NORM_EPS)) * gain_ref[...]
        up = pltpu.roll(tn, aw - HEAD_DIM // 2, axis=1)
        dn = pltpu.roll(tn, HEAD_DIM // 2, axis=1)
        return tn * cos + jnp.where(first_half, up, dn) * sin

    q = qk_norm_rope(proj(off, aw), qg_ref)
    q_ref[...] = (q * q_scale).astype(q_ref.dtype)
    off += aw
    k = qk_norm_rope(proj(off, aw), kg_ref)
    off += aw
    v = proj(off, aw)
    off += aw
    if blocks_per_tile:
        k_ref[0] = k.T
        v_ref[0] = v.T
    else:
        k_ref[...] = k
        v_ref[...] = v
    z_ref[...] = proj(off, sw)
    off += sw
    xbc_ref[...] = proj(off, xw)
    dt_ref[...] = jnp.dot(hn, wdt_ref[...], preferred_element_type=F32)
    gate_ref[...] = _sigmoid(jnp.dot(hn, wg_ref[...], preferred_element_type=F32)).astype(BF16)
    if blocks_per_tile:
        kb_ref, vb_ref, km_ref = block_refs
        kb_ref[...] = k.astype(BF16)
        vb_ref[...] = v.astype(BF16)
        for c in range(blocks_per_tile):
            km_ref[0, c:c + 1, :] = jnp.mean(k[c * MOBA_BLOCK:(c + 1) * MOBA_BLOCK], axis=0, keepdims=True)


def _rope_tables(pos, n_heads):
    half = HEAD_DIM // 2
    inv = jnp.power(ROPE_THETA, -jnp.arange(half, dtype=F32) * (2.0 / HEAD_DIM))
    ang = pos.astype(F32)[:, None] * inv[None, :]
    cos = jnp.cos(ang)
    sin = jnp.sin(ang)
    cos_h = jnp.concatenate([cos, cos], axis=-1)
    sin_h = jnp.concatenate([-sin, sin], axis=-1)
    return jnp.tile(cos_h, (1, n_heads)), jnp.tile(sin_h, (1, n_heads))


def _head_mean_matrix(width):
    idx = np.arange(width) // HEAD_DIM
    return jnp.asarray((idx[:, None] == idx[None, :]).astype(np.float32) / HEAD_DIM, dtype=BF16)


def _project(x2d, pw, cos, sin, dims, tile_rows, seq_tiles, block_stats):
    d_model, cw, aw, sw, xw, _ = dims
    n = x2d.shape[0]
    tm = tile_rows
    assert n % tm == 0
    bpt = tm // MOBA_BLOCK if block_stats else 0
    assert not block_stats or tm % MOBA_BLOCK == 0
    n_gate = pw['wg'].shape[1]
    row = lambda w: pl.BlockSpec((tm, w), lambda i: (i, 0))
    tab = pl.BlockSpec((tm, aw), lambda i: (i % seq_tiles, 0))
    in_specs = [row(d_model), _resident((1, d_model)), _resident(pw['wm'].shape), _resident(pw['wdt'].shape),
                _resident(pw['wg'].shape), tab, tab, _resident((1, aw)), _resident((1, aw)), _resident((aw, aw))]
    sds = jax.ShapeDtypeStruct
    out_shape = [sds((n, cw), F32), sds((n, aw), BF16 if block_stats else F32), sds((n, aw), F32),
                 sds((n, aw), F32), sds((n, sw), F32), sds((n, xw), F32), sds((n, LANES), F32),
                 sds((n, n_gate), BF16)]
    out_specs = [row(cw), row(aw), row(aw), row(aw), row(sw), row(xw), row(LANES), row(n_gate)]
    if block_stats:
        seq_len = seq_tiles * tm
        kv_t = pl.BlockSpec((1, aw, tm), lambda i: (i // seq_tiles, 0, i % seq_tiles))
        out_shape[2] = out_shape[3] = sds((n // seq_len, aw, seq_len), F32)
        out_specs[2] = out_specs[3] = kv_t
        out_shape += [sds((n, aw), BF16), sds((n, aw), BF16), sds((n // tm, bpt, aw), F32)]
        out_specs += [row(aw), row(aw), pl.BlockSpec((1, bpt, aw), lambda i: (i, 0, 0))]
    q_scale = HEAD_DIM ** -0.5 * (LOG2_E if block_stats else 1.0)
    return pl.pallas_call(
        functools.partial(_proj_kernel, widths=(cw, aw, sw, xw), blocks_per_tile=bpt, q_scale=q_scale),
        grid=(n // tm,), in_specs=in_specs, out_specs=out_specs, out_shape=out_shape,
        compiler_params=_cparams(("parallel",)), name="in_proj",
    )(x2d, pw['g1'], pw['wm'], pw['wdt'], pw['wg'], cos, sin, pw['qg'], pw['kg'], pw['pm'])


CONV_HALO = 32
CONV_ROWS = 32


def _conv_seq_kernel(u_ref, w_ref, b_ref, lg_ref, lb_ref, o_ref, ext_ref, sh_ref, *, tile):
    @pl.when(pl.program_id(1) == 0)
    def _():
        ext_ref[0:CONV_HALO, :] = jnp.zeros((CONV_HALO, ext_ref.shape[1]), F32)

    ext_ref[CONV_HALO:CONV_HALO + tile, :] = u_ref[0]
    rows = CONV_HALO + tile
    ext = ext_ref[...]
    sh_ref[0] = ext
    for b in range(1, SUBLANES):
        sh_ref[b] = pltpu.roll(ext, rows - b, axis=0)
    base = CONV_HALO - (CONV_WIDTH - 1)
    for r0 in range(0, tile, CONV_ROWS):
        acc = None
        for kk in range(CONV_WIDTH):
            j = base + kk
            lo = r0 + (j // SUBLANES) * SUBLANES
            term = sh_ref[j % SUBLANES, lo:lo + CONV_ROWS, :] * w_ref[kk:kk + 1, :]
            acc = term if acc is None else acc + term
        y = acc + b_ref[...]
        mu = jnp.mean(y, axis=-1, keepdims=True)
        yc = y - mu
        var = jnp.mean(yc * yc, axis=-1, keepdims=True)
        yn = (yc * lax.rsqrt(var + NORM_EPS)) * lg_ref[...] + lb_ref[...]
        o_ref[0, r0:r0 + CONV_ROWS, :] = _silu(yn).astype(o_ref.dtype)
    ext_ref[0:CONV_HALO, :] = ext_ref[tile:tile + CONV_HALO, :]


def _conv_branch_seq(u, lw, tile):
    bsz, L, c = u.shape
    assert L % tile == 0 and tile >= CONV_HALO
    vec = lambda a: a.reshape(1, c)
    return pl.pallas_call(
        functools.partial(_conv_seq_kernel, tile=tile),
        grid=(bsz, L // tile),
        in_specs=[pl.BlockSpec((1, tile, c), lambda b, l: (b, l, 0)), _resident((CONV_WIDTH, c)),
                  _resident((1, c)), _resident((1, c)), _resident((1, c))],
        out_specs=pl.BlockSpec((1, tile, c), lambda b, l: (b, l, 0)),
        out_shape=jax.ShapeDtypeStruct((bsz, L, c), BF16),
        scratch_shapes=[pltpu.VMEM((CONV_HALO + tile, c), F32), pltpu.VMEM((SUBLANES, CONV_HALO + tile, c), F32)],
        compiler_params=_cparams(("parallel", "arbitrary")), name="conv_branch",
    )(u, lw['conv_w'], vec(lw['conv_b']), vec(lw['conv_ln_g']), vec(lw['conv_ln_b']))


def _moba_seq_kernel(q_ref, k_ref, v_ref, km_ref, blockmask_ref, o_ref, sa_ref, sb_ref, m_ref, l_ref, acc_ref,
                     *, n_blocks, chunk_blocks):
    i = pl.program_id(2)
    blk = MOBA_BLOCK
    ck = chunk_blocks * blk
    nh = HEADS_PER_LANE_TILE
    q_pair = q_ref[0]
    lane = lax.broadcasted_iota(jnp.int32, (blk, LANES), 1)
    blk_id = lax.broadcasted_iota(jnp.int32, (n_blocks, blk), 0)
    km_parts = _split3(km_ref[0])

    q_aug = []
    for h in range(nh):
        qh = jnp.where((lane // HEAD_DIM) == h, q_pair, jnp.zeros_like(q_pair))
        gate = _dot_nt(km_parts[0], qh) + _dot_nt(km_parts[1], qh) + _dot_nt(km_parts[2], qh)
        gate = jnp.where(blk_id < i, gate, -jnp.inf)
        unpicked = jnp.ones((n_blocks, blk), F32)
        for _ in range(MOBA_TOPK):
            best = jnp.max(gate, axis=0, keepdims=True)
            idx = jnp.min(jnp.where(gate == best, blk_id, n_blocks), axis=0, keepdims=True)
            hit = blk_id == idx
            unpicked = jnp.where(hit, 0.0, unpicked)
            gate = jnp.where(hit, -jnp.inf, gate)
        unpicked = jnp.concatenate([unpicked, jnp.zeros((LANES - n_blocks, blk), F32)], axis=0)
        q_aug.append(jnp.concatenate([qh, unpicked.T.astype(BF16)], axis=1))

    qa = jnp.concatenate(q_aug, axis=0)
    own = pl.multiple_of(i * blk, blk)
    row = lax.broadcasted_iota(jnp.int32, (nh * blk, blk), 0) % blk
    col = lax.broadcasted_iota(jnp.int32, (nh * blk, blk), 1)
    s = jnp.where(col <= row, _dot_nt(qa[:, 0:LANES], k_ref[0, pl.ds(own, blk), :]), NEG_BIG)
    m = jnp.max(s, axis=-1, keepdims=True)
    p = jnp.exp2(s - m)
    m_ref[...] = m
    l_ref[...] = jnp.sum(p, axis=-1, keepdims=True)
    acc_ref[...] = jnp.dot(p.astype(BF16), v_ref[0, pl.ds(own, blk), :], preferred_element_type=F32)

    last_chunk = n_blocks // chunk_blocks - 1

    def score_into(dst_ref, c):
        start = pl.multiple_of(jnp.minimum(c, last_chunk) * ck, ck)
        k_aug = jnp.concatenate([k_ref[0, pl.ds(start, ck), :], blockmask_ref[pl.ds(start, ck), :]], axis=1)
        dst_ref[...] = _dot_nt(qa, k_aug)

    def absorb(src_ref, c):
        s = src_ref[...]
        m = m_ref[...]
        m_new = jnp.maximum(m, jnp.max(s, axis=-1, keepdims=True))
        alpha = jnp.exp2(m - m_new)
        p = jnp.exp2(s - m_new)
        m_ref[...] = m_new
        l_ref[...] = l_ref[...] * alpha + jnp.sum(p, axis=-1, keepdims=True)
        v_c = v_ref[0, pl.ds(pl.multiple_of(c * ck, ck), ck), :]
        acc_ref[...] = acc_ref[...] * alpha + jnp.dot(p.astype(BF16), v_c, preferred_element_type=F32)

    def body(c2, carry):
        c = 2 * c2
        score_into(sb_ref, c + 1)
        absorb(sa_ref, c)
        score_into(sa_ref, c + 2)
        absorb(sb_ref, c + 1)
        return carry

    n_chunks = (i + chunk_blocks - 1) // chunk_blocks
    score_into(sa_ref, 0)
    lax.fori_loop(0, (n_chunks + 1) // 2, body, 0)
    out = acc_ref[...] / l_ref[...]
    o_ref[0] = jnp.where(lane < HEAD_DIM, out[0:blk], out[blk:2 * blk]).astype(o_ref.dtype)


MOBA_CHUNK_BLOCKS = 2


def _moba_seq(q, k, v, kmean):
    bsz, L, w = q.shape
    nb = L // MOBA_BLOCK
    assert L % MOBA_BLOCK == 0 and w % LANES == 0 and nb <= LANES and HEADS_PER_LANE_TILE == 2
    cb = MOBA_CHUNK_BLOCKS if nb % (2 * MOBA_CHUNK_BLOCKS) == 0 else 1
    assert (nb // cb) % 2 == 0
    rows = HEADS_PER_LANE_TILE * MOBA_BLOCK
    score_buf = pltpu.VMEM((rows, cb * MOBA_BLOCK), F32)
    blockmask = np.zeros((L, LANES), np.float32)
    blockmask[np.arange(L), np.arange(L) // MOBA_BLOCK] = NEG_BIG
    kv_spec = pl.BlockSpec((1, L, LANES), lambda b, hp, i: (b, 0, hp))
    return pl.pallas_call(
        functools.partial(_moba_seq_kernel, n_blocks=nb, chunk_blocks=cb),
        grid=(bsz, w // LANES, nb),
        in_specs=[pl.BlockSpec((1, MOBA_BLOCK, LANES), lambda b, hp, i: (b, i, hp)), kv_spec, kv_spec,
                  pl.BlockSpec((1, nb, LANES), lambda b, hp, i: (b, 0, hp)), _resident((L, LANES))],
        out_specs=pl.BlockSpec((1, MOBA_BLOCK, LANES), lambda b, hp, i: (b, i, hp)),
        out_shape=jax.ShapeDtypeStruct((bsz, L, w), BF16),
        scratch_shapes=[score_buf, score_buf, pltpu.VMEM((rows, 1), F32), pltpu.VMEM((rows, 1), F32),
                        pltpu.VMEM((rows, LANES), F32)],
        compiler_params=_cparams(("parallel", "parallel", "arbitrary")), name="moba_prompt",
    )(q, k, v, kmean, jnp.asarray(blockmask, dtype=BF16))


SSM_HALO = SUBLANES


def _ssd_seq_kernel(xbc_ref, dt_ref, z_ref, cw_ref, cb_ref, dtb_ref, a_ref, dsk_ref, ng_ref, tri_ref,
                    o_ref, h_ref, ext_ref, y_ref, *, tile, inner, n_pairs):
    q = SSD_CHUNK
    gs = SSM_GROUPS * SSM_STATE

    @pl.when(pl.program_id(1) == 0)
    def _():
        ext_ref[0:SSM_HALO, :] = jnp.zeros((SSM_HALO, ext_ref.shape[1]), F32)
        h_ref[...] = jnp.zeros_like(h_ref)

    ext_ref[SSM_HALO:SSM_HALO + tile, :] = xbc_ref[0]
    base = SSM_HALO - (SSM_CONV - 1)
    acc = ext_ref[base:base + tile, :] * cw_ref[0:1, :]
    for kk in range(1, SSM_CONV):
        acc = acc + ext_ref[base + kk:base + kk + tile, :] * cw_ref[kk:kk + 1, :]
    xc = _silu(acc + cb_ref[...])
    ext_ref[0:SSM_HALO, :] = ext_ref[tile:tile + SSM_HALO, :]

    dt = _softplus(dt_ref[0] + dtb_ref[...])
    da = dt * a_ref[...]
    tri = tri_ref[...]
    rows = lax.broadcasted_iota(jnp.int32, (q, q), 0)
    cols = lax.broadcasted_iota(jnp.int32, (q, q), 1)
    lower = cols <= rows
    lane = lax.broadcasted_iota(jnp.int32, (q, LANES), 1)
    left = lane < SSM_HEAD_DIM
    top = lax.broadcasted_iota(jnp.int32, (LANES, SSM_STATE), 0) < SSM_HEAD_DIM

    for c in range(tile // q):
        r0 = c * q
        hi, mid, lo = _split3(da[r0:r0 + q])
        cum = (jnp.dot(tri, hi, preferred_element_type=F32) + jnp.dot(tri, mid, preferred_element_type=F32)
               + jnp.dot(tri, lo, preferred_element_type=F32))
        cum_t = cum.T
        e_cum = jnp.exp(cum)
        e_rest = jnp.exp(cum[q - 1:q, :] - cum)
        e_last_t = jnp.exp(cum_t[:, q - 1:q])
        dt_c = dt[r0:r0 + q]
        bm = xc[r0:r0 + q, inner:inner + gs]
        cm = xc[r0:r0 + q, inner + gs:inner + 2 * gs]
        for p in range(n_pairs):
            g = (p * HEADS_PER_LANE_TILE * SSM_HEAD_DIM) // (inner // SSM_GROUPS)
            b_g = bm[:, g * SSM_STATE:(g + 1) * SSM_STATE].astype(BF16)
            c_g = cm[:, g * SSM_STATE:(g + 1) * SSM_STATE].astype(BF16)
            cb = _dot_nt(c_g, b_g)
            h0, h1 = HEADS_PER_LANE_TILE * p, HEADS_PER_LANE_TILE * p + 1
            xs = xc[r0:r0 + q, p * LANES:(p + 1) * LANES]
            xdt = xs * jnp.where(left, dt_c[:, h0:h0 + 1], dt_c[:, h1:h1 + 1])
            xdt_b = xdt.astype(BF16)
            y_pair = None
            for hh, head in enumerate((h0, h1)):
                seg = cum[:, head:head + 1] - cum_t[head:head + 1, :]
                lmat = jnp.where(lower, jnp.exp(seg), 0.0)
                y_h = jnp.dot((cb * lmat).astype(BF16), xdt_b, preferred_element_type=F32)
                y_pair = y_h if hh == 0 else jnp.where(left, y_pair, y_h)
            h_pair = h_ref[0, p * LANES:(p + 1) * LANES, :]
            inter = _dot_nt(c_g, h_pair.astype(BF16))
            y_pair = y_pair + inter * jnp.where(left, e_cum[:, h0:h0 + 1], e_cum[:, h1:h1 + 1])
            y_ref[r0:r0 + q, p * LANES:(p + 1) * LANES] = y_pair
            xdd = xdt * jnp.where(left, e_rest[:, h0:h0 + 1], e_rest[:, h1:h1 + 1])
            upd = jnp.dot(xdd.T.astype(BF16), b_g, preferred_element_type=F32)
            keep = jnp.where(top, e_last_t[h0:h0 + 1, :], e_last_t[h1:h1 + 1, :])
            h_ref[0, p * LANES:(p + 1) * LANES, :] = h_pair * keep + upd

    y = y_ref[...] + dsk_ref[...] * xc[:, 0:inner]
    gated = y * _silu(z_ref[0])
    o_ref[0] = _rmsnorm_rows(gated, ng_ref[...]).astype(o_ref.dtype)


def _ssd_seq(xbc, dt_raw, z, lw, tile):
    bsz, L, xw = xbc.shape
    inner = z.shape[-1]
    n_heads = inner // SSM_HEAD_DIM
    assert L % tile == 0 and tile % SSD_CHUNK == 0 and inner % LANES == 0
    assert (inner // SSM_GROUPS) % LANES == 0
    pad_h = lambda a: jnp.pad(a, (0, LANES - n_heads)).reshape(1, LANES)
    tri = jnp.asarray(np.tril(np.ones((SSD_CHUNK, SSD_CHUNK), np.float32)), dtype=BF16)
    seq = lambda w: pl.BlockSpec((1, tile, w), lambda b, l: (b, l, 0))
    return pl.pallas_call(
        functools.partial(_ssd_seq_kernel, tile=tile, inner=inner, n_pairs=inner // LANES),
        grid=(bsz, L // tile),
        in_specs=[seq(xw), seq(LANES), seq(inner), _resident((SSM_CONV, xw)), _resident((1, xw)),
                  _resident((1, LANES)), _resident((1, LANES)), _resident((1, inner)), _resident((1, inner)),
                  _resident((SSD_CHUNK, SSD_CHUNK))],
        out_specs=[seq(inner), pl.BlockSpec((1, inner, SSM_STATE), lambda b, l: (b, 0, 0))],
        out_shape=[jax.ShapeDtypeStruct((bsz, L, inner), BF16),
                   jax.ShapeDtypeStruct((bsz, inner, SSM_STATE), F32)],
        scratch_shapes=[pltpu.VMEM((SSM_HALO + tile, xw), F32), pltpu.VMEM((tile, inner), F32)],
        compiler_params=_cparams(("parallel", "arbitrary")), name="ssd_scan",
    )(xbc, dt_raw, z, lw['ssm_conv_w'], lw['ssm_conv_b'].reshape(1, xw), pad_h(lw['dt_bias']),
      pad_h(-jnp.exp(lw['a_log'])), jnp.repeat(lw['d_skip'], SSM_HEAD_DIM).reshape(1, inner),
      lw['ssm_norm_g'].reshape(1, inner), tri)


def _merge_kernel(x_ref, oc_ref, oa_ref, os_ref, gate_ref, wc_ref, wa_ref, ws_ref, wo_ref, o_ref):
    d = x_ref.shape[1]
    gates = gate_ref[...]
    branch = lambda o, w: jnp.dot(o[...], w[...], preferred_element_type=F32)
    merged = (gates[:, 0:d].astype(F32) * branch(oc_ref, wc_ref)
              + gates[:, d:2 * d].astype(F32) * branch(oa_ref, wa_ref)
              + gates[:, 2 * d:3 * d].astype(F32) * branch(os_ref, ws_ref))
    o_ref[...] = x_ref[...] + jnp.dot(merged.astype(BF16), wo_ref[...], preferred_element_type=F32)


def _merge(x2d, o_conv, o_att, o_ssm, gates, mw, tile_rows):
    n, d = x2d.shape
    tm = tile_rows
    assert n % tm == 0
    row = lambda w: pl.BlockSpec((tm, w), lambda i: (i, 0))
    bw = o_conv.shape[1]
    return pl.pallas_call(
        _merge_kernel, grid=(n // tm,),
        in_specs=[row(d), row(bw), row(bw), row(bw), row(3 * d), _resident((bw, d)), _resident((bw, d)),
                  _resident((bw, d)), _resident((d, d))],
        out_specs=row(d), out_shape=jax.ShapeDtypeStruct((n, d), F32),
        compiler_params=_cparams(("parallel",)), name="merge_out_proj",
    )(x2d, o_conv, o_att, o_ssm, gates, mw['wc'], mw['wa'], mw['ws'], mw['wo'])


def _add_block_key_sums(pages, ksum_ref, first_block, pages_per_block):
    width = ksum_ref.shape[0]
    lane = lax.broadcasted_iota(jnp.int32, ksum_ref.shape, 1)
    acc = ksum_ref[...]
    for r in range(len(pages) // pages_per_block):
        tile = pages[r * pages_per_block][0, 0].reshape(width, -1)
        for t in range(1, pages_per_block):
            tile = tile + pages[r * pages_per_block + t][0, 0].reshape(width, -1)
        acc = jnp.where(lane == first_block + r, jnp.sum(tile, axis=-1, keepdims=True), acc)
    ksum_ref[...] = acc


def _ffn_kernel(*refs, chunk, n_pages_step, steps_per_seq, pages_per_block):
    if n_pages_step:
        refs = refs[1:]
    x_ref, g2_ref, wa_ref, wg_ref, wo_ref = refs[:5]
    pages, o_ref = refs[5:5 + n_pages_step], refs[5 + n_pages_step]
    x = x_ref[...]
    hn = _rmsnorm_rows(x, g2_ref[...]).astype(BF16)
    acc = x
    for c in range(wa_ref.shape[1] // chunk):
        sl = slice(c * chunk, (c + 1) * chunk)
        a = jnp.dot(hn, wa_ref[:, sl], preferred_element_type=F32)
        g = jnp.dot(hn, wg_ref[:, sl], preferred_element_type=F32)
        acc = acc + jnp.dot((_silu(a) * g).astype(BF16), wo_ref[sl, :], preferred_element_type=F32)
    o_ref[...] = acc
    if n_pages_step:
        ksum_ref = refs[6 + n_pages_step]
        c = lax.rem(pl.program_id(0), steps_per_seq)

        @pl.when(c == 0)
        def _():
            ksum_ref[...] = jnp.zeros_like(ksum_ref)

        _add_block_key_sums(pages, ksum_ref.at[0], c * (n_pages_step // pages_per_block), pages_per_block)


def _ffn_chunk(hidden):
    best = LANES
    for c in range(LANES, hidden + 1, LANES):
        if hidden % c == 0 and c <= 1536:
            best = c
    return best


def _ffn(x2d, fw, tile_rows, paged_keys=None):
    n, d = x2d.shape
    tm = tile_rows
    hidden = fw['wa'].shape[1]
    n_steps = n // tm
    assert n % tm == 0 and hidden % LANES == 0
    weights = [_resident((1, d)), _resident((d, hidden)), _resident((d, hidden)), _resident((hidden, d))]
    operands = (x2d, fw['g2'], fw['wa'], fw['wg'], fw['wo'])
    if paged_keys is None:
        row = pl.BlockSpec((tm, d), lambda i: (i, 0))
        return pl.pallas_call(
            functools.partial(_ffn_kernel, chunk=_ffn_chunk(hidden), n_pages_step=0, steps_per_seq=1,
                              pages_per_block=1),
            grid=(n_steps,), in_specs=[row] + weights, out_specs=row,
            out_shape=jax.ShapeDtypeStruct((n, d), F32),
            compiler_params=_cparams(("parallel",)), name="swiglu_ffn",
        )(*operands)

    cache_t, page_table, layer = paged_keys
    _, _, n_heads, hd, page = cache_t.shape
    s, n_pages = page_table.shape
    ppb = MOBA_BLOCK // page
    assert MOBA_BLOCK % page == 0 and (s * n_pages) % n_steps == 0
    pps = s * n_pages // n_steps
    assert n_pages % pps == 0 and pps % ppb == 0 and n_pages // ppb <= LANES
    sps = n_pages // pps
    row = pl.BlockSpec((tm, d), lambda i, pt: (i, 0))

    def page_spec(r):
        return pl.BlockSpec((1, 1, n_heads, hd, page),
                            lambda i, pt: (layer, pt[(i // sps) * n_pages + (i % sps) * pps + r], 0, 0, 0))

    grid_spec = pltpu.PrefetchScalarGridSpec(
        num_scalar_prefetch=1, grid=(n_steps,),
        in_specs=[row] + weights + [page_spec(r) for r in range(pps)],
        out_specs=[row, pl.BlockSpec((1, n_heads * hd, LANES), lambda i, pt: (i // sps, 0, 0))])
    return pl.pallas_call(
        functools.partial(_ffn_kernel, chunk=_ffn_chunk(hidden), n_pages_step=pps, steps_per_seq=sps,
                          pages_per_block=ppb),
        grid_spec=grid_spec,
        out_shape=[jax.ShapeDtypeStruct((n, d), F32), jax.ShapeDtypeStruct((s, n_heads * hd, LANES), F32)],
        compiler_params=_cparams(("arbitrary",)), name="swiglu_ffn_key_sums",
    )(page_table.reshape(-1), *operands, *([cache_t] * pps))


def _decode_mix_kernel(u_ref, cst_ref, cw_ref, cb_ref, lg_ref, lb_ref, xbc_ref, sst_ref, sw_ref, sb_ref,
                       dt_ref, dtb_ref, a_ref, ex_ref,
                       oc_ref, xs_ref, bm_ref, cm_ref, xdt_ref, dec_ref, *, inner):
    gs = SSM_GROUPS * SSM_STATE
    acc = u_ref[...] * cw_ref[CONV_WIDTH - 1:CONV_WIDTH, :] + cb_ref[...]
    for kk in range(CONV_WIDTH - 1):
        acc = acc + cst_ref[kk] * cw_ref[kk:kk + 1, :]
    mu = jnp.mean(acc, axis=-1, keepdims=True)
    yc = acc - mu
    var = jnp.mean(yc * yc, axis=-1, keepdims=True)
    oc_ref[...] = _silu((yc * lax.rsqrt(var + NORM_EPS)) * lg_ref[...] + lb_ref[...]).astype(oc_ref.dtype)

    acc = xbc_ref[...] * sw_ref[SSM_CONV - 1:SSM_CONV, :] + sb_ref[...]
    for kk in range(SSM_CONV - 1):
        acc = acc + sst_ref[kk] * sw_ref[kk:kk + 1, :]
    xc = _silu(acc)
    xs = xc[:, 0:inner]
    xs_ref[...] = xs
    bm_ref[...] = xc[:, inner:inner + gs]
    cm_ref[...] = xc[:, inner + gs:inner + 2 * gs]
    dt = _softplus(dt_ref[...] + dtb_ref[...])
    dec = jnp.exp(dt * a_ref[...])
    xdt_ref[...] = xs * _dot_f32_by_exact(dt, ex_ref[...])
    dec_ref[...] = _dot_f32_by_exact(dec, ex_ref[...])


def _decode_mix(u, conv_state, xbc, ssm_conv_state, dt_raw, lw, inner):
    s, cw = u.shape
    xw = xbc.shape[1]
    n_heads = inner // SSM_HEAD_DIM
    gs = SSM_GROUPS * SSM_STATE
    pad_h = lambda a: jnp.pad(a, (0, LANES - n_heads)).reshape(1, LANES)
    expand = np.zeros((LANES, inner), np.float32)
    for h in range(n_heads):
        expand[h, h * SSM_HEAD_DIM:(h + 1) * SSM_HEAD_DIM] = 1.0
    sds = jax.ShapeDtypeStruct
    return pl.pallas_call(
        functools.partial(_decode_mix_kernel, inner=inner),
        out_shape=[sds((s, cw), BF16), sds((s, inner), F32), sds((s, gs), F32), sds((s, gs), F32),
                   sds((s, inner), F32), sds((s, inner), F32)],
        compiler_params=_cparams(), name="decode_conv_ssm_inputs",
    )(u, jnp.swapaxes(conv_state, 0, 1), lw['conv_w'], lw['conv_b'].reshape(1, cw),
      lw['conv_ln_g'].reshape(1, cw), lw['conv_ln_b'].reshape(1, cw),
      xbc, jnp.swapaxes(ssm_conv_state, 0, 1), lw['ssm_conv_w'], lw['ssm_conv_b'].reshape(1, xw),
      dt_raw, pad_h(lw['dt_bias']), pad_h(-jnp.exp(lw['a_log'])), jnp.asarray(expand, dtype=BF16))


def _decode_state_kernel(h_ref, xcol_ref, dcol_ref, xdt_ref, dec_ref, xs_ref, z_ref, bm_ref, cm_ref,
                         dsk_ref, ng_ref, hn_ref, o_ref, *, inner):
    half = inner // SSM_GROUPS
    h = h_ref[0]
    row = lax.broadcasted_iota(jnp.int32, h.shape, 0)
    b0 = bm_ref[0][:, 0:SSM_STATE]
    b1 = bm_ref[0][:, SSM_STATE:2 * SSM_STATE]
    c0 = cm_ref[0][:, 0:SSM_STATE]
    c1 = cm_ref[0][:, SSM_STATE:2 * SSM_STATE]
    hn_ref[0] = h * dcol_ref[0] + xcol_ref[0] * jnp.where(row < half, b0, b1)

    def c_dot_h(c_row, h_rows):
        ch, cm_, _ = _split3(jnp.broadcast_to(c_row, (SUBLANES, SSM_STATE)))
        hh, hm, _ = _split3(h_rows)
        return (_dot_nt(ch, hh) + _dot_nt(ch, hm) + _dot_nt(cm_, hh))[0:1, :]

    ch = jnp.concatenate([c_dot_h(c0, h[0:half]), c_dot_h(c1, h[half:inner])], axis=1)
    lane = lax.broadcasted_iota(jnp.int32, (1, inner), 1)
    cb = jnp.where(lane < half, jnp.sum(c0 * b0, axis=-1, keepdims=True), jnp.sum(c1 * b1, axis=-1, keepdims=True))
    y = ch * dec_ref[0] + cb * xdt_ref[0] + dsk_ref[...] * xs_ref[0]
    gated = y * _silu(z_ref[0])
    o_ref[0] = _rmsnorm_rows(gated, ng_ref[...]).astype(o_ref.dtype)


def _decode_state(h0, xdt, dec, xs, z, bm, cm, lw, inner):
    s = h0.shape[0]
    gs = SSM_GROUPS * SSM_STATE
    per = lambda shape: pl.BlockSpec((1,) + shape, lambda b: (b, 0, 0))
    r3 = lambda a: a.reshape(s, 1, a.shape[-1])
    return pl.pallas_call(
        functools.partial(_decode_state_kernel, inner=inner), grid=(s,),
        in_specs=[per((inner, SSM_STATE)), per((inner, 1)), per((inner, 1)), per((1, inner)), per((1, inner)),
                  per((1, inner)), per((1, inner)), per((1, gs)), per((1, gs)), _resident((1, inner)),
                  _resident((1, inner))],
        out_specs=[per((inner, SSM_STATE)), per((1, inner))],
        out_shape=[jax.ShapeDtypeStruct((s, inner, SSM_STATE), F32), jax.ShapeDtypeStruct((s, 1, inner), BF16)],
        compiler_params=_cparams(("parallel",)), name="decode_ssm_state",
    )(h0, xdt.reshape(s, inner, 1), dec.reshape(s, inner, 1), r3(xdt), r3(dec), r3(xs), r3(z), r3(bm), r3(cm),
      jnp.repeat(lw['d_skip'], SSM_HEAD_DIM).reshape(1, inner), lw['ssm_norm_g'].reshape(1, inner))


def _dot_f32_pair(a, b, nt=False):
    f = _dot_nt if nt else (lambda x, y: jnp.dot(x, y, preferred_element_type=F32))
    a_hi, a_mid, _ = _split3(a)
    b_hi, b_mid, _ = _split3(b)
    return f(a_hi, b_hi) + f(a_hi, b_mid) + f(a_mid, b_hi)


def _gate_topk_kernel(q_ref, ksum_ref, o_ref, *, n_blocks):
    n_heads = o_ref.shape[1]
    width = ksum_ref.shape[1]
    head_of_lane = lax.broadcasted_iota(jnp.int32, (n_heads, width), 1) // HEAD_DIM
    head_of_row = lax.broadcasted_iota(jnp.int32, (n_heads, width), 0)
    q_rows = jnp.where(head_of_lane == head_of_row, q_ref[0], 0.0)
    gate = _dot_f32_pair(q_rows, ksum_ref[0])
    blk = lax.broadcasted_iota(jnp.int32, gate.shape, 1)
    gate = jnp.where(blk < n_blocks, gate, -jnp.inf)
    out = jnp.zeros(gate.shape, jnp.int32)
    for r in range(MOBA_TOPK):
        best = jnp.max(gate, axis=-1, keepdims=True)
        idx = jnp.min(jnp.where(gate == best, blk, n_blocks - 1), axis=-1, keepdims=True)
        out = jnp.where(blk == r, idx, out)
        gate = jnp.where(blk == idx, -jnp.inf, gate)
    o_ref[0] = out


def _gate_topk(q, ksum, n_heads, n_blocks):
    s, w = q.shape
    assert MOBA_TOPK <= n_blocks <= LANES
    out = pl.pallas_call(
        functools.partial(_gate_topk_kernel, n_blocks=n_blocks), grid=(s,),
        in_specs=[pl.BlockSpec((1, 1, w), lambda b: (b, 0, 0)), pl.BlockSpec((1, w, LANES), lambda b: (b, 0, 0))],
        out_specs=pl.BlockSpec((1, n_heads, LANES), lambda b: (b, 0, 0)),
        out_shape=jax.ShapeDtypeStruct((s, n_heads, LANES), jnp.int32),
        compiler_params=_cparams(("parallel",)), name="decode_gate_topk",
    )(q.reshape(s, 1, w), ksum)
    return out[:, :, :MOBA_TOPK]


def _moba_decode_kernel(pt_ref, pick_ref, q_ref, kn_ref, vn_ref, *refs, n_pages_sel):
    del pt_ref, pick_ref
    o_ref = refs[-1]
    kp, vp = refs[:n_pages_sel], refs[n_pages_sel:2 * n_pages_sel]
    q = q_ref[0, 0]
    q_rows = jnp.broadcast_to(q, (SUBLANES, q.shape[1]))
    s_own = jnp.sum(kn_ref[0, 0] * q, axis=-1, keepdims=True)
    ss = [_dot_f32_pair(q_rows, kr[0, 0, 0])[0:1, :] for kr in kp]
    m = s_own
    for s in ss:
        m = jnp.maximum(m, jnp.max(s, axis=-1, keepdims=True))
    p_own = jnp.exp(s_own - m)
    den = p_own
    acc = p_own * vn_ref[0, 0]
    for s, vr in zip(ss, vp):
        p = jnp.exp(s - m)
        den = den + jnp.sum(p, axis=-1, keepdims=True)
        acc = acc + _dot_f32_pair(jnp.broadcast_to(p, (SUBLANES, p.shape[1])), vr[0, 0, 0], nt=True)[0:1, :]
    o_ref[0, 0] = acc / den


def _moba_decode(q, k_new, v_new, cache_kt, cache_vt, page_table, picks, layer):
    s, w = q.shape
    _, _, n_heads, hd, page = cache_kt.shape
    n_pages = page_table.shape[1]
    ppb = MOBA_BLOCK // page
    n_sel = MOBA_TOPK * ppb

    def page_spec(r, t):
        def imap(b, h, pt, pk):
            blk = pk[(b * n_heads + h) * MOBA_TOPK + r]
            return (layer, pt[b * n_pages + blk * ppb + t], h, 0, 0)
        return pl.BlockSpec((1, 1, 1, hd, page), imap)

    vec = pl.BlockSpec((1, 1, 1, hd), lambda b, h, pt, pk: (b, h, 0, 0))
    page_specs = [page_spec(r, t) for r in range(MOBA_TOPK) for t in range(ppb)]
    grid_spec = pltpu.PrefetchScalarGridSpec(
        num_scalar_prefetch=2, grid=(s, n_heads), in_specs=[vec, vec, vec] + page_specs + page_specs,
        out_specs=vec)
    rows = lambda a: a.reshape(s, n_heads, 1, hd)
    out = pl.pallas_call(
        functools.partial(_moba_decode_kernel, n_pages_sel=n_sel), grid_spec=grid_spec,
        out_shape=jax.ShapeDtypeStruct((s, n_heads, 1, hd), F32),
        compiler_params=_cparams(("parallel", "parallel")), name="moba_decode",
    )(page_table.reshape(-1), picks.reshape(-1), rows(q), rows(k_new), rows(v_new),
      *([cache_kt] * n_sel), *([cache_vt] * n_sel))
    return out.reshape(s, w)


PROJ_TILE = 256
CONV_TILE = 256
SSD_TILE = 256
MERGE_TILE = 512
FFN_TILE = 256


def _prep_layer_weights(lw, dims):
    d_model, cw, aw, sw, xw, n_ssm_heads = dims
    w_in = lw['w_in']
    main = 2 * cw + 3 * aw + sw + xw
    n_heads = aw // HEAD_DIM
    hidden = lw['w_ffn_out'].shape[0]
    proj = {
        'g1': lw['norm1_g'].reshape(1, d_model),
        'wm': w_in[:, :main].astype(BF16),
        'wdt': jnp.pad(w_in[:, main:main + n_ssm_heads], ((0, 0), (0, LANES - n_ssm_heads))).astype(BF16),
        'wg': w_in[:, main + n_ssm_heads:].astype(BF16),
        'qg': jnp.tile(lw['q_norm_g'], n_heads).reshape(1, aw),
        'kg': jnp.tile(lw['k_norm_g'], n_heads).reshape(1, aw),
        'pm': _head_mean_matrix(aw),
    }
    merge = {'wc': lw['w_branch_conv'].astype(BF16), 'wa': lw['w_branch_att'].astype(BF16),
             'ws': lw['w_branch_ssm'].astype(BF16), 'wo': lw['w_out'].astype(BF16)}
    ffn = {'g2': lw['norm2_g'].reshape(1, d_model), 'wa': lw['w_ffn_in'][:, :hidden].astype(BF16),
           'wg': lw['w_ffn_in'][:, hidden:].astype(BF16), 'wo': lw['w_ffn_out'].astype(BF16)}
    return proj, merge, ffn


def kernel(x_prompt, x_sample, cache_k, cache_v, state_conv, state_ssm_conv, state_ssm, page_table,
           norm1_g, w_in, conv_w, conv_b, conv_ln_g, conv_ln_b, q_norm_g, k_norm_g, ssm_conv_w, ssm_conv_b,
           dt_bias, a_log, d_skip, ssm_norm_g, w_branch_conv, w_branch_att, w_branch_ssm, w_out, norm2_g,
           w_ffn_in, w_ffn_out):
    bp, lp, d_model = x_prompt.shape
    bs, ls, _ = x_sample.shape
    depth, n_pool, page, att_heads, head_dim = cache_k.shape
    assert head_dim == HEAD_DIM and ls == 1
    cw = conv_w.shape[-1]
    aw = att_heads * HEAD_DIM
    sw = ssm_norm_g.shape[-1]
    xw = ssm_conv_w.shape[-1]
    n_ssm_heads = a_log.shape[-1]
    assert xw == sw + 2 * SSM_GROUPS * SSM_STATE and sw == n_ssm_heads * SSM_HEAD_DIM
    assert conv_w.shape[1] == CONV_WIDTH and ssm_conv_w.shape[1] == SSM_CONV
    dims = (d_model, cw, aw, sw, xw, n_ssm_heads)
    past_len = page_table.shape[1] * page
    assert past_len % MOBA_BLOCK == 0 and lp % PROJ_TILE == 0

    cos_p, sin_p = _rope_tables(jnp.arange(lp, dtype=jnp.int32), att_heads)
    cos_s, sin_s = _rope_tables(jnp.full((bs,), past_len, jnp.int32), att_heads)
    cache_kt = jnp.transpose(cache_k, (0, 1, 3, 4, 2))
    cache_vt = jnp.transpose(cache_v, (0, 1, 3, 4, 2))

    hp = x_prompt.reshape(bp * lp, d_model)
    hs = x_sample.reshape(bs, d_model)
    outs = {name: [] for name in ('kp', 'vp', 'cp', 'scp', 'sp', 'ks', 'vs', 'cs', 'scs', 'ss')}
    for l in range(depth):
        lw = {'w_in': w_in[l], 'norm1_g': norm1_g[l], 'q_norm_g': q_norm_g[l], 'k_norm_g': k_norm_g[l],
              'conv_w': conv_w[l], 'conv_b': conv_b[l], 'conv_ln_g': conv_ln_g[l], 'conv_ln_b': conv_ln_b[l],
              'ssm_conv_w': ssm_conv_w[l], 'ssm_conv_b': ssm_conv_b[l], 'dt_bias': dt_bias[l],
              'a_log': a_log[l], 'd_skip': d_skip[l], 'ssm_norm_g': ssm_norm_g[l],
              'w_branch_conv': w_branch_conv[l], 'w_branch_att': w_branch_att[l],
              'w_branch_ssm': w_branch_ssm[l], 'w_out': w_out[l], 'norm2_g': norm2_g[l],
              'w_ffn_in': w_ffn_in[l], 'w_ffn_out': w_ffn_out[l]}
        pw, mw, fw = _prep_layer_weights(lw, dims)

        u, q, k, v, z, xbc, dt_raw, gates, k_b, v_b, kmean = _project(
            hp, pw, cos_p, sin_p, dims, PROJ_TILE, lp // PROJ_TILE, block_stats=True)
        seq = lambda a: a.reshape(bp, lp, a.shape[-1])
        o_conv = _conv_branch_seq(seq(u), lw, CONV_TILE)
        o_att = _moba_seq(seq(q), seq(k_b), seq(v_b), kmean.reshape(bp, lp // MOBA_BLOCK, aw))
        o_ssm, h_fin = _ssd_seq(seq(xbc), seq(dt_raw), seq(z), lw, SSD_TILE)
        flat = lambda a: a.reshape(bp * lp, a.shape[-1])
        hp = _merge(hp, flat(o_conv), flat(o_att), flat(o_ssm), gates, mw, MERGE_TILE)
        hp, key_sums = _ffn(hp, fw, FFN_TILE, paged_keys=(cache_kt, page_table, l))
        by_pos = lambda a: jnp.transpose(a.reshape(bp, att_heads, HEAD_DIM, lp), (0, 3, 1, 2))
        outs['kp'].append(by_pos(k))
        outs['vp'].append(by_pos(v))
        outs['cp'].append(seq(u)[:, lp - (CONV_WIDTH - 1):])
        outs['scp'].append(seq(xbc)[:, lp - (SSM_CONV - 1):])
        outs['sp'].append(h_fin.reshape(bp, n_ssm_heads, SSM_HEAD_DIM, SSM_STATE))

        u, q, k, v, z, xbc, dt_raw, gates = _project(hs, pw, cos_s, sin_s, dims, bs, 1, block_stats=False)
        o_conv, xs, bm, cm, xdt, dec = _decode_mix(u, state_conv[l], xbc, state_ssm_conv[l], dt_raw, lw, sw)
        h_new, o_ssm = _decode_state(state_ssm[l].reshape(bs, sw, SSM_STATE), xdt, dec, xs, z, bm, cm, lw, sw)
        picks = _gate_topk(q, key_sums, att_heads, past_len // MOBA_BLOCK)
        o_att = _moba_decode(q, k, v, cache_kt, cache_vt, page_table, picks, l)
        hs = _merge(hs, o_conv, o_att.astype(BF16), o_ssm.reshape(bs, sw), gates, mw, bs)
        hs = _ffn(hs, fw, bs)
        outs['ks'].append(k.reshape(bs, 1, att_heads, HEAD_DIM))
        outs['vs'].append(v.reshape(bs, 1, att_heads, HEAD_DIM))
        outs['cs'].append(jnp.concatenate([state_conv[l][:, 1:], u[:, None, :]], axis=1))
        outs['scs'].append(jnp.concatenate([state_ssm_conv[l][:, 1:], xbc[:, None, :]], axis=1))
        outs['ss'].append(h_new.reshape(bs, n_ssm_heads, SSM_HEAD_DIM, SSM_STATE))

    st = lambda name: jnp.stack(outs[name])
    return (hp.reshape(bp, lp, d_model), hs.reshape(bs, 1, d_model),
            st('kp'), st('vp'), st('cp'), st('scp'), st('sp'),
            st('ks'), st('vs'), st('cs'), st('scs'), st('ss'))
```

```python
import functools

import numpy as np
import jax
import jax.numpy as jnp
from jax import lax
from jax.experimental import pallas as pl
from jax.experimental.pallas import tpu as pltpu

F32 = jnp.float32
BF16 = jnp.bfloat16

CONV_WIDTH = 31
HEAD_DIM = 64
MOBA_BLOCK = 256
MOBA_TOPK = 3
ROPE_THETA = 10000.0
SSM_HEAD_DIM = 64
SSM_GROUPS = 2
SSM_STATE = 128
SSM_CONV = 4
SSD_CHUNK = 128
NORM_EPS = 1e-6

LANES = 128
SUBLANES = 8
VMEM_LIMIT_BYTES = 56 * 1024 * 1024

NEG_BIG = -1e30
LOG2_E = 1.4426950408889634
HEADS_PER_LANE_TILE = LANES // HEAD_DIM


def _cparams(sem=None):
    return pltpu.CompilerParams(dimension_semantics=sem, vmem_limit_bytes=VMEM_LIMIT_BYTES)


def _resident(shape):
    nd = len(shape)
    return pl.BlockSpec(shape, lambda *_: (0,) * nd, pipeline_mode=pl.Buffered(1))


def _split3(x):
    hi = x.astype(BF16)
    r1 = x - hi.astype(F32)
    mid = r1.astype(BF16)
    lo = (r1 - mid.astype(F32)).astype(BF16)
    return hi, mid, lo


def _dot_nt(a, b):
    return lax.dot_general(a, b, (((1,), (1,)), ((), ())), preferred_element_type=F32)


def _dot_f32_by_exact(a, b_bf16, nt=False):
    f = (lambda p: _dot_nt(p, b_bf16)) if nt else (lambda p: jnp.dot(p, b_bf16, preferred_element_type=F32))
    hi, mid, lo = _split3(a)
    return f(hi) + f(mid) + f(lo)


def _sigmoid(x):
    return 1.0 / (1.0 + jnp.exp(-x))


def _silu(x):
    return x * _sigmoid(x)


def _softplus(x):
    return jnp.maximum(x, 0.0) + jnp.log(1.0 + jnp.exp(-jnp.abs(x)))


def _rmsnorm_rows(x, gain):
    return (x * lax.rsqrt(jnp.mean(x * x, axis=-1, keepdims=True) + NORM_EPS)) * gain


def _proj_kernel(x_ref, g1_ref, wm_ref, wdt_ref, wg_ref, cos_ref, sin_ref, qg_ref, kg_ref, pm_ref,
                 u_ref, q_ref, k_ref, v_ref, z_ref, xbc_ref, dt_ref, gate_ref, *block_refs,
                 widths, blocks_per_tile, q_scale):
    cw, aw, sw, xw = widths
    x = x_ref[...]
    hn = _rmsnorm_rows(x, g1_ref[...]).astype(BF16)

    def proj(lo, width):
        return jnp.dot(hn, wm_ref[:, lo:lo + width], preferred_element_type=F32)

    off = 0
    a = proj(off, cw)
    g = proj(off + cw, cw)
    u_ref[...] = a * _sigmoid(g)
    off += 2 * cw

    lane = lax.broadcasted_iota(jnp.int32, (x.shape[0], aw), 1)
    first_half = (lane % HEAD_DIM) < (HEAD_DIM // 2)
    cos = cos_ref[...]
    sin = sin_ref[...]

    def qk_norm_rope(t, gain_ref):
        ss = jnp.dot((t * t).astype(BF16), pm_ref[...], preferred_element_type=F32)
        tn = (t * lax.rsqrt(ss + NORM_EPS)) * gain_ref[...]
        up = pltpu.roll(tn, aw - HEAD_DIM // 2, axis=1)
        dn = pltpu.roll(tn, HEAD_DIM // 2, axis=1)
        return tn * cos + jnp.where(first_half, up, dn) * sin

    q = qk_norm_rope(proj(off, aw), qg_ref)
    q_ref[...] = (q * q_scale).astype(q_ref.dtype)
    off += aw
    k = qk_norm_rope(proj(off, aw), kg_ref)
    off += aw
    v = proj(off, aw)
    off += aw
    if blocks_per_tile:
        k_ref[0] = k.T
        v_ref[0] = v.T
    else:
        k_ref[...] = k
        v_ref[...] = v
    z_ref[...] = proj(off, sw)
    off += sw
    xbc_ref[...] = proj(off, xw)
    dt_ref[...] = jnp.dot(hn, wdt_ref[...], preferred_element_type=F32)
    gate_ref[...] = _sigmoid(jnp.dot(hn, wg_ref[...], preferred_element_type=F32)).astype(BF16)
    if blocks_per_tile:
        kb_ref, vb_ref, km_ref = block_refs
        kb_ref[...] = k.astype(BF16)
        vb_ref[...] = v.astype(BF16)
        for c in range(blocks_per_tile):
            km_ref[0, c:c + 1, :] = jnp.mean(k[c * MOBA_BLOCK:(c + 1) * MOBA_BLOCK], axis=0, keepdims=True)


def _rope_tables(pos, n_heads):
    half = HEAD_DIM // 2
    inv = jnp.power(ROPE_THETA, -jnp.arange(half, dtype=F32) * (2.0 / HEAD_DIM))
    ang = pos.astype(F32)[:, None] * inv[None, :]
    cos = jnp.cos(ang)
    sin = jnp.sin(ang)
    cos_h = jnp.concatenate([cos, cos], axis=-1)
    sin_h = jnp.concatenate([-sin, sin], axis=-1)
    return jnp.tile(cos_h, (1, n_heads)), jnp.tile(sin_h, (1, n_heads))


def _head_mean_matrix(width):
    idx = np.arange(width) // HEAD_DIM
    return jnp.asarray((idx[:, None] == idx[None, :]).astype(np.float32) / HEAD_DIM, dtype=BF16)


def _project(x2d, pw, cos, sin, dims, tile_rows, seq_tiles, block_stats):
    d_model, cw, aw, sw, xw, _ = dims
    n = x2d.shape[0]
    tm = tile_rows
    assert n % tm == 0
    bpt = tm // MOBA_BLOCK if block_stats else 0
    assert not block_stats or tm % MOBA_BLOCK == 0
    n_gate = pw['wg'].shape[1]
    row = lambda w: pl.BlockSpec((tm, w), lambda i: (i, 0))
    tab = pl.BlockSpec((tm, aw), lambda i: (i % seq_tiles, 0))
    in_specs = [row(d_model), _resident((1, d_model)), _resident(pw['wm'].shape), _resident(pw['wdt'].shape),
                _resident(pw['wg'].shape), tab, tab, _resident((1, aw)), _resident((1, aw)), _resident((aw, aw))]
    sds = jax.ShapeDtypeStruct
    out_shape = [sds((n, cw), F32), sds((n, aw), BF16 if block_stats else F32), sds((n, aw), F32),
                 sds((n, aw), F32), sds((n, sw), F32), sds((n, xw), F32), sds((n, LANES), F32),
                 sds((n, n_gate), BF16)]
    out_specs = [row(cw), row(aw), row(aw), row(aw), row(sw), row(xw), row(LANES), row(n_gate)]
    if block_stats:
        seq_len = seq_tiles * tm
        kv_t = pl.BlockSpec((1, aw, tm), lambda i: (i // seq_tiles, 0, i % seq_tiles))
        out_shape[2] = out_shape[3] = sds((n // seq_len, aw, seq_len), F32)
        out_specs[2] = out_specs[3] = kv_t
        out_shape += [sds((n, aw), BF16), sds((n, aw), BF16), sds((n // tm, bpt, aw), F32)]
        out_specs += [row(aw), row(aw), pl.BlockSpec((1, bpt, aw), lambda i: (i, 0, 0))]
    q_scale = HEAD_DIM ** -0.5 * (LOG2_E if block_stats else 1.0)
    return pl.pallas_call(
        functools.partial(_proj_kernel, widths=(cw, aw, sw, xw), blocks_per_tile=bpt, q_scale=q_scale),
        grid=(n // tm,), in_specs=in_specs, out_specs=out_specs, out_shape=out_shape,
        compiler_params=_cparams(("parallel",)), name="in_proj",
    )(x2d, pw['g1'], pw['wm'], pw['wdt'], pw['wg'], cos, sin, pw['qg'], pw['kg'], pw['pm'])


CONV_HALO = 32
CONV_ROWS = 32


def _conv_seq_kernel(u_ref, w_ref, b_ref, lg_ref, lb_ref, o_ref, ext_ref, sh_ref, *, tile):
    @pl.when(pl.program_id(1) == 0)
    def _():
        ext_ref[0:CONV_HALO, :] = jnp.zeros((CONV_HALO, ext_ref.shape[1]), F32)

    ext_ref[CONV_HALO:CONV_HALO + tile, :] = u_ref[0]
    rows = CONV_HALO + tile
    ext = ext_ref[...]
    sh_ref[0] = ext
    for b in range(1, SUBLANES):
        sh_ref[b] = pltpu.roll(ext, rows - b, axis=0)
    base = CONV_HALO - (CONV_WIDTH - 1)
    for r0 in range(0, tile, CONV_ROWS):
        acc = None
        for kk in range(CONV_WIDTH):
            j = base + kk
            lo = r0 + (j // SUBLANES) * SUBLANES
            term = sh_ref[j % SUBLANES, lo:lo + CONV_ROWS, :] * w_ref[kk:kk + 1, :]
            acc = term if acc is None else acc + term
        y = acc + b_ref[...]
        mu = jnp.mean(y, axis=-1, keepdims=True)
        yc = y - mu
        var = jnp.mean(yc * yc, axis=-1, keepdims=True)
        yn = (yc * lax.rsqrt(var + NORM_EPS)) * lg_ref[...] + lb_ref[...]
        o_ref[0, r0:r0 + CONV_ROWS, :] = _silu(yn).astype(o_ref.dtype)
    ext_ref[0:CONV_HALO, :] = ext_ref[tile:tile + CONV_HALO, :]


def _conv_branch_seq(u, lw, tile):
    bsz, L, c = u.shape
    assert L % tile == 0 and tile >= CONV_HALO
    vec = lambda a: a.reshape(1, c)
    return pl.pallas_call(
        functools.partial(_conv_seq_kernel, tile=tile),
        grid=(bsz, L // tile),
        in_specs=[pl.BlockSpec((1, tile, c), lambda b, l: (b, l, 0)), _resident((CONV_WIDTH, c)),
                  _resident((1, c)), _resident((1, c)), _resident((1, c))],
        out_specs=pl.BlockSpec((1, tile, c), lambda b, l: (b, l, 0)),
        out_shape=jax.ShapeDtypeStruct((bsz, L, c), BF16),
        scratch_shapes=[pltpu.VMEM((CONV_HALO + tile, c), F32), pltpu.VMEM((SUBLANES, CONV_HALO + tile, c), F32)],
        compiler_params=_cparams(("parallel", "arbitrary")), name="conv_branch",
    )(u, lw['conv_w'], vec(lw['conv_b']), vec(lw['conv_ln_g']), vec(lw['conv_ln_b']))


def _moba_seq_kernel(q_ref, k_ref, v_ref, km_ref, blockmask_ref, o_ref, sa_ref, sb_ref, *, n_blocks,
                     chunk_blocks):
    i = pl.program_id(2)
    blk = MOBA_BLOCK
    ck = chunk_blocks * blk
    nh = HEADS_PER_LANE_TILE
    q_pair = q_ref[0]
    lane = lax.broadcasted_iota(jnp.int32, (blk, LANES), 1)
    blk_id = lax.broadcasted_iota(jnp.int32, (n_blocks, blk), 0)
    km_parts = _split3(km_ref[0])

    q_aug = []
    for h in range(nh):
        qh = jnp.where((lane // HEAD_DIM) == h, q_pair, jnp.zeros_like(q_pair))
        gate = _dot_nt(km_parts[0], qh) + _dot_nt(km_parts[1], qh) + _dot_nt(km_parts[2], qh)
        gate = jnp.where(blk_id < i, gate, -jnp.inf)
        unpicked = jnp.ones((n_blocks, blk), F32)
        for _ in range(MOBA_TOPK):
            best = jnp.max(gate, axis=0, keepdims=True)
            idx = jnp.min(jnp.where(gate == best, blk_id, n_blocks), axis=0, keepdims=True)
            hit = blk_id == idx
            unpicked = jnp.where(hit, 0.0, unpicked)
            gate = jnp.where(hit, -jnp.inf, gate)
        unpicked = jnp.concatenate([unpicked, jnp.zeros((LANES - n_blocks, blk), F32)], axis=0)
        q_aug.append(jnp.concatenate([qh, unpicked.T.astype(BF16)], axis=1))

    qa = jnp.concatenate(q_aug, axis=0)
    own = pl.multiple_of(i * blk, blk)
    row = lax.broadcasted_iota(jnp.int32, (nh * blk, blk), 0) % blk
    col = lax.broadcasted_iota(jnp.int32, (nh * blk, blk), 1)
    s = jnp.where(col <= row, _dot_nt(qa[:, 0:LANES], k_ref[0, pl.ds(own, blk), :]), NEG_BIG)
    m = jnp.max(s, axis=-1, keepdims=True)
    p = jnp.exp2(s - m)
    l = jnp.sum(p, axis=-1, keepdims=True)
    acc = jnp.dot(p.astype(BF16), v_ref[0, pl.ds(own, blk), :], preferred_element_type=F32)

    last_chunk = n_blocks // chunk_blocks - 1

    def score_into(dst_ref, c):
        start = pl.multiple_of(jnp.minimum(c, last_chunk) * ck, ck)
        k_aug = jnp.concatenate([k_ref[0, pl.ds(start, ck), :], blockmask_ref[pl.ds(start, ck), :]], axis=1)
        dst_ref[...] = _dot_nt(qa, k_aug)

    def absorb(src_ref, c, m, l, acc):
        s = src_ref[...]
        m_new = jnp.maximum(m, jnp.max(s, axis=-1, keepdims=True))
        alpha = jnp.exp2(m - m_new)
        p = jnp.exp2(s - m_new)
        l = l * alpha + jnp.sum(p, axis=-1, keepdims=True)
        v_c = v_ref[0, pl.ds(pl.multiple_of(c * ck, ck), ck), :]
        return m_new, l, acc * alpha + jnp.dot(p.astype(BF16), v_c, preferred_element_type=F32)

    def body(c2, carry):
        m, l, acc = carry
        c = 2 * c2
        score_into(sb_ref, c + 1)
        m, l, acc = absorb(sa_ref, c, m, l, acc)
        score_into(sa_ref, c + 2)
        return absorb(sb_ref, c + 1, m, l, acc)

    n_chunks = (i + chunk_blocks - 1) // chunk_blocks
    score_into(sa_ref, 0)
    m, l, acc = lax.fori_loop(0, (n_chunks + 1) // 2, body, (m, l, acc))
    out = acc / l
    o_ref[0] = jnp.where(lane < HEAD_DIM, out[0:blk], out[blk:2 * blk]).astype(o_ref.dtype)


MOBA_CHUNK_BLOCKS = 2


def _moba_seq(q, k, v, kmean):
    bsz, L, w = q.shape
    nb = L // MOBA_BLOCK
    assert L % MOBA_BLOCK == 0 and w % LANES == 0 and nb <= LANES and HEADS_PER_LANE_TILE == 2
    cb = MOBA_CHUNK_BLOCKS if nb % (2 * MOBA_CHUNK_BLOCKS) == 0 else 1
    assert (nb // cb) % 2 == 0
    score_buf = pltpu.VMEM((HEADS_PER_LANE_TILE * MOBA_BLOCK, cb * MOBA_BLOCK), F32)
    blockmask = np.zeros((L, LANES), np.float32)
    blockmask[np.arange(L), np.arange(L) // MOBA_BLOCK] = NEG_BIG
    kv_spec = pl.BlockSpec((1, L, LANES), lambda b, hp, i: (b, 0, hp))
    return pl.pallas_call(
        functools.partial(_moba_seq_kernel, n_blocks=nb, chunk_blocks=cb),
        grid=(bsz, w // LANES, nb),
        in_specs=[pl.BlockSpec((1, MOBA_BLOCK, LANES), lambda b, hp, i: (b, i, hp)), kv_spec, kv_spec,
                  pl.BlockSpec((1, nb, LANES), lambda b, hp, i: (b, 0, hp)), _resident((L, LANES))],
        out_specs=pl.BlockSpec((1, MOBA_BLOCK, LANES), lambda b, hp, i: (b, i, hp)),
        out_shape=jax.ShapeDtypeStruct((bsz, L, w), BF16), scratch_shapes=[score_buf, score_buf],
        compiler_params=_cparams(("parallel", "parallel", "arbitrary")), name="moba_prompt",
    )(q, k, v, kmean, jnp.asarray(blockmask, dtype=BF16))


SSM_HALO = SUBLANES


def _ssd_seq_kernel(xbc_ref, dt_ref, z_ref, cw_ref, cb_ref, dtb_ref, a_ref, dsk_ref, ng_ref, tri_ref,
                    o_ref, h_ref, ext_ref, y_ref, *, tile, inner, n_pairs):
    q = SSD_CHUNK
    gs = SSM_GROUPS * SSM_STATE

    @pl.when(pl.program_id(1) == 0)
    def _():
        ext_ref[0:SSM_HALO, :] = jnp.zeros((SSM_HALO, ext_ref.shape[1]), F32)
        h_ref[...] = jnp.zeros_like(h_ref)

    ext_ref[SSM_HALO:SSM_HALO + tile, :] = xbc_ref[0]
    base = SSM_HALO - (SSM_CONV - 1)
    acc = ext_ref[base:base + tile, :] * cw_ref[0:1, :]
    for kk in range(1, SSM_CONV):
        acc = acc + ext_ref[base + kk:base + kk + tile, :] * cw_ref[kk:kk + 1, :]
    xc = _silu(acc + cb_ref[...])
    ext_ref[0:SSM_HALO, :] = ext_ref[tile:tile + SSM_HALO, :]

    dt = _softplus(dt_ref[0] + dtb_ref[...])
    da = dt * a_ref[...]
    tri = tri_ref[...]
    rows = lax.broadcasted_iota(jnp.int32, (q, q), 0)
    cols = lax.broadcasted_iota(jnp.int32, (q, q), 1)
    lower = cols <= rows
    lane = lax.broadcasted_iota(jnp.int32, (q, LANES), 1)
    left = lane < SSM_HEAD_DIM
    top = lax.broadcasted_iota(jnp.int32, (LANES, SSM_STATE), 0) < SSM_HEAD_DIM

    for c in range(tile // q):
        r0 = c * q
        hi, mid, lo = _split3(da[r0:r0 + q])
        cum = (jnp.dot(tri, hi, preferred_element_type=F32) + jnp.dot(tri, mid, preferred_element_type=F32)
               + jnp.dot(tri, lo, preferred_element_type=F32))
        cum_t = cum.T
        e_cum = jnp.exp(cum)
        e_rest = jnp.exp(cum[q - 1:q, :] - cum)
        e_last_t = jnp.exp(cum_t[:, q - 1:q])
        dt_c = dt[r0:r0 + q]
        bm = xc[r0:r0 + q, inner:inner + gs]
        cm = xc[r0:r0 + q, inner + gs:inner + 2 * gs]
        for p in range(n_pairs):
            g = (p * HEADS_PER_LANE_TILE * SSM_HEAD_DIM) // (inner // SSM_GROUPS)
            b_g = bm[:, g * SSM_STATE:(g + 1) * SSM_STATE].astype(BF16)
            c_g = cm[:, g * SSM_STATE:(g + 1) * SSM_STATE].astype(BF16)
            cb = _dot_nt(c_g, b_g)
            h0, h1 = HEADS_PER_LANE_TILE * p, HEADS_PER_LANE_TILE * p + 1
            xs = xc[r0:r0 + q, p * LANES:(p + 1) * LANES]
            xdt = xs * jnp.where(left, dt_c[:, h0:h0 + 1], dt_c[:, h1:h1 + 1])
            xdt_b = xdt.astype(BF16)
            y_pair = None
            for hh, head in enumerate((h0, h1)):
                seg = cum[:, head:head + 1] - cum_t[head:head + 1, :]
                lmat = jnp.where(lower, jnp.exp(seg), 0.0)
                y_h = jnp.dot((cb * lmat).astype(BF16), xdt_b, preferred_element_type=F32)
                y_pair = y_h if hh == 0 else jnp.where(left, y_pair, y_h)
            h_pair = h_ref[0, p * LANES:(p + 1) * LANES, :]
            inter = _dot_nt(c_g, h_pair.astype(BF16))
            y_pair = y_pair + inter * jnp.where(left, e_cum[:, h0:h0 + 1], e_cum[:, h1:h1 + 1])
            y_ref[r0:r0 + q, p * LANES:(p + 1) * LANES] = y_pair
            xdd = xdt * jnp.where(left, e_rest[:, h0:h0 + 1], e_rest[:, h1:h1 + 1])
            upd = jnp.dot(xdd.T.astype(BF16), b_g, preferred_element_type=F32)
            keep = jnp.where(top, e_last_t[h0:h0 + 1, :], e_last_t[h1:h1 + 1, :])
            h_ref[0, p * LANES:(p + 1) * LANES, :] = h_pair * keep + upd

    y = y_ref[...] + dsk_ref[...] * xc[:, 0:inner]
    gated = y * _silu(z_ref[0])
    o_ref[0] = _rmsnorm_rows(gated, ng_ref[...]).astype(o_ref.dtype)


def _ssd_seq(xbc, dt_raw, z, lw, tile):
    bsz, L, xw = xbc.shape
    inner = z.shape[-1]
    n_heads = inner // SSM_HEAD_DIM
    assert L % tile == 0 and tile % SSD_CHUNK == 0 and inner % LANES == 0
    assert (inner // SSM_GROUPS) % LANES == 0
    pad_h = lambda a: jnp.pad(a, (0, LANES - n_heads)).reshape(1, LANES)
    tri = jnp.asarray(np.tril(np.ones((SSD_CHUNK, SSD_CHUNK), np.float32)), dtype=BF16)
    seq = lambda w: pl.BlockSpec((1, tile, w), lambda b, l: (b, l, 0))
    return pl.pallas_call(
        functools.partial(_ssd_seq_kernel, tile=tile, inner=inner, n_pairs=inner // LANES),
        grid=(bsz, L // tile),
        in_specs=[seq(xw), seq(LANES), seq(inner), _resident((SSM_CONV, xw)), _resident((1, xw)),
                  _resident((1, LANES)), _resident((1, LANES)), _resident((1, inner)), _resident((1, inner)),
                  _resident((SSD_CHUNK, SSD_CHUNK))],
        out_specs=[seq(inner), pl.BlockSpec((1, inner, SSM_STATE), lambda b, l: (b, 0, 0))],
        out_shape=[jax.ShapeDtypeStruct((bsz, L, inner), BF16),
                   jax.ShapeDtypeStruct((bsz, inner, SSM_STATE), F32)],
        scratch_shapes=[pltpu.VMEM((SSM_HALO + tile, xw), F32), pltpu.VMEM((tile, inner), F32)],
        compiler_params=_cparams(("parallel", "arbitrary")), name="ssd_scan",
    )(xbc, dt_raw, z, lw['ssm_conv_w'], lw['ssm_conv_b'].reshape(1, xw), pad_h(lw['dt_bias']),
      pad_h(-jnp.exp(lw['a_log'])), jnp.repeat(lw['d_skip'], SSM_HEAD_DIM).reshape(1, inner),
      lw['ssm_norm_g'].reshape(1, inner), tri)


def _merge_kernel(x_ref, oc_ref, oa_ref, os_ref, gate_ref, wc_ref, wa_ref, ws_ref, wo_ref, o_ref):
    d = x_ref.shape[1]
    gates = gate_ref[...]
    branch = lambda o, w: jnp.dot(o[...], w[...], preferred_element_type=F32)
    merged = (gates[:, 0:d].astype(F32) * branch(oc_ref, wc_ref)
              + gates[:, d:2 * d].astype(F32) * branch(oa_ref, wa_ref)
              + gates[:, 2 * d:3 * d].astype(F32) * branch(os_ref, ws_ref))
    o_ref[...] = x_ref[...] + jnp.dot(merged.astype(BF16), wo_ref[...], preferred_element_type=F32)


def _merge(x2d, o_conv, o_att, o_ssm, gates, mw, tile_rows):
    n, d = x2d.shape
    tm = tile_rows
    assert n % tm == 0
    row = lambda w: pl.BlockSpec((tm, w), lambda i: (i, 0))
    bw = o_conv.shape[1]
    return pl.pallas_call(
        _merge_kernel, grid=(n // tm,),
        in_specs=[row(d), row(bw), row(bw), row(bw), row(3 * d), _resident((bw, d)), _resident((bw, d)),
                  _resident((bw, d)), _resident((d, d))],
        out_specs=row(d), out_shape=jax.ShapeDtypeStruct((n, d), F32),
        compiler_params=_cparams(("parallel",)), name="merge_out_proj",
    )(x2d, o_conv, o_att, o_ssm, gates, mw['wc'], mw['wa'], mw['ws'], mw['wo'])


def _add_block_gates(q_row, pages, gate_ref, first_block, pages_per_block):
    n_heads = gate_ref.shape[0]
    width = q_row.shape[1]
    head_of_lane = lax.broadcasted_iota(jnp.int32, (n_heads, width), 1) // HEAD_DIM
    head_of_row = lax.broadcasted_iota(jnp.int32, (n_heads, width), 0)
    q_rows = jnp.where(head_of_lane == head_of_row, q_row, 0.0)
    lane = lax.broadcasted_iota(jnp.int32, gate_ref.shape, 1)
    acc = jnp.zeros(gate_ref.shape, F32)
    for r in range(len(pages) // pages_per_block):
        tile = pages[r * pages_per_block][0, 0].reshape(width, -1)
        for t in range(1, pages_per_block):
            tile = tile + pages[r * pages_per_block + t][0, 0].reshape(width, -1)
        per_pos = _dot_f32_pair(q_rows, tile)
        acc = jnp.where(lane == first_block + r, jnp.sum(per_pos, axis=-1, keepdims=True), acc)
    gate_ref[...] = acc


def _ffn_kernel(*refs, chunk, n_pages_step, steps_per_seq, pages_per_block):
    if n_pages_step:
        refs = refs[1:]
        sample_q_ref, refs = refs[5], refs[:5] + refs[6:]
    x_ref, g2_ref, wa_ref, wg_ref, wo_ref = refs[:5]
    pages, o_ref = refs[5:5 + n_pages_step], refs[5 + n_pages_step]
    x = x_ref[...]
    hn = _rmsnorm_rows(x, g2_ref[...]).astype(BF16)
    acc = x
    for c in range(wa_ref.shape[1] // chunk):
        sl = slice(c * chunk, (c + 1) * chunk)
        a = jnp.dot(hn, wa_ref[:, sl], preferred_element_type=F32)
        g = jnp.dot(hn, wg_ref[:, sl], preferred_element_type=F32)
        acc = acc + jnp.dot((_silu(a) * g).astype(BF16), wo_ref[sl, :], preferred_element_type=F32)
    o_ref[...] = acc
    if n_pages_step:
        gate_ref = refs[6 + n_pages_step]
        c = lax.rem(pl.program_id(0), steps_per_seq)
        _add_block_gates(sample_q_ref[0], pages, gate_ref.at[0, 0], c * (n_pages_step // pages_per_block),
                         pages_per_block)


def _ffn_chunk(hidden):
    best = LANES
    for c in range(LANES, hidden + 1, LANES):
        if hidden % c == 0 and c <= 1536:
            best = c
    return best


def _ffn(x2d, fw, tile_rows, paged_keys=None):
    n, d = x2d.shape
    tm = tile_rows
    hidden = fw['wa'].shape[1]
    n_steps = n // tm
    assert n % tm == 0 and hidden % LANES == 0
    weights = [_resident((1, d)), _resident((d, hidden)), _resident((d, hidden)), _resident((hidden, d))]
    operands = (x2d, fw['g2'], fw['wa'], fw['wg'], fw['wo'])
    if paged_keys is None:
        row = pl.BlockSpec((tm, d), lambda i: (i, 0))
        return pl.pallas_call(
            functools.partial(_ffn_kernel, chunk=_ffn_chunk(hidden), n_pages_step=0, steps_per_seq=1,
                              pages_per_block=1),
            grid=(n_steps,), in_specs=[row] + weights, out_specs=row,
            out_shape=jax.ShapeDtypeStruct((n, d), F32),
            compiler_params=_cparams(("parallel",)), name="swiglu_ffn",
        )(*operands)

    cache_t, page_table, layer, sample_q = paged_keys
    _, _, n_heads, hd, page = cache_t.shape
    s, n_pages = page_table.shape
    ppb = MOBA_BLOCK // page
    assert MOBA_BLOCK % page == 0 and (s * n_pages) % n_steps == 0
    pps = s * n_pages // n_steps
    assert n_pages % pps == 0 and pps % ppb == 0 and n_pages // ppb <= LANES
    sps = n_pages // pps
    row = pl.BlockSpec((tm, d), lambda i, pt: (i, 0))

    def page_spec(r):
        return pl.BlockSpec((1, 1, n_heads, hd, page),
                            lambda i, pt: (layer, pt[(i // sps) * n_pages + (i % sps) * pps + r], 0, 0, 0))

    w = n_heads * hd
    q_spec = pl.BlockSpec((1, 1, w), lambda i, pt: (i // sps, 0, 0))
    grid_spec = pltpu.PrefetchScalarGridSpec(
        num_scalar_prefetch=1, grid=(n_steps,),
        in_specs=[row] + weights + [q_spec] + [page_spec(r) for r in range(pps)],
        out_specs=[row, pl.BlockSpec((1, 1, n_heads, LANES), lambda i, pt: (i // sps, i % sps, 0, 0))])
    return pl.pallas_call(
        functools.partial(_ffn_kernel, chunk=_ffn_chunk(hidden), n_pages_step=pps, steps_per_seq=sps,
                          pages_per_block=ppb),
        grid_spec=grid_spec,
        out_shape=[jax.ShapeDtypeStruct((n, d), F32), jax.ShapeDtypeStruct((s, sps, n_heads, LANES), F32)],
        compiler_params=_cparams(("arbitrary",)), name="swiglu_ffn_paged_gates",
    )(page_table.reshape(-1), *operands, sample_q.reshape(s, 1, w), *([cache_t] * pps))


def _decode_mix_kernel(u_ref, cst_ref, cw_ref, cb_ref, lg_ref, lb_ref, xbc_ref, sst_ref, sw_ref, sb_ref,
                       dt_ref, dtb_ref, a_ref, ex_ref,
                       oc_ref, xs_ref, bm_ref, cm_ref, xdt_ref, dec_ref, *, inner):
    gs = SSM_GROUPS * SSM_STATE
    acc = u_ref[...] * cw_ref[CONV_WIDTH - 1:CONV_WIDTH, :] + cb_ref[...]
    for kk in range(CONV_WIDTH - 1):
        acc = acc + cst_ref[kk] * cw_ref[kk:kk + 1, :]
    mu = jnp.mean(acc, axis=-1, keepdims=True)
    yc = acc - mu
    var = jnp.mean(yc * yc, axis=-1, keepdims=True)
    oc_ref[...] = _silu((yc * lax.rsqrt(var + NORM_EPS)) * lg_ref[...] + lb_ref[...]).astype(oc_ref.dtype)

    acc = xbc_ref[...] * sw_ref[SSM_CONV - 1:SSM_CONV, :] + sb_ref[...]
    for kk in range(SSM_CONV - 1):
        acc = acc + sst_ref[kk] * sw_ref[kk:kk + 1, :]
    xc = _silu(acc)
    xs = xc[:, 0:inner]
    xs_ref[...] = xs
    bm_ref[...] = xc[:, inner:inner + gs]
    cm_ref[...] = xc[:, inner + gs:inner + 2 * gs]
    dt = _softplus(dt_ref[...] + dtb_ref[...])
    dec = jnp.exp(dt * a_ref[...])
    xdt_ref[...] = xs * _dot_f32_by_exact(dt, ex_ref[...])
    dec_ref[...] = _dot_f32_by_exact(dec, ex_ref[...])


def _decode_mix(u, conv_state, xbc, ssm_conv_state, dt_raw, lw, inner):
    s, cw = u.shape
    xw = xbc.shape[1]
    n_heads = inner // SSM_HEAD_DIM
    gs = SSM_GROUPS * SSM_STATE
    pad_h = lambda a: jnp.pad(a, (0, LANES - n_heads)).reshape(1, LANES)
    expand = np.zeros((LANES, inner), np.float32)
    for h in range(n_heads):
        expand[h, h * SSM_HEAD_DIM:(h + 1) * SSM_HEAD_DIM] = 1.0
    sds = jax.ShapeDtypeStruct
    return pl.pallas_call(
        functools.partial(_decode_mix_kernel, inner=inner),
        out_shape=[sds((s, cw), BF16), sds((s, inner), F32), sds((s, gs), F32), sds((s, gs), F32),
                   sds((s, inner), F32), sds((s, inner), F32)],
        compiler_params=_cparams(), name="decode_conv_ssm_inputs",
    )(u, jnp.swapaxes(conv_state, 0, 1), lw['conv_w'], lw['conv_b'].reshape(1, cw),
      lw['conv_ln_g'].reshape(1, cw), lw['conv_ln_b'].reshape(1, cw),
      xbc, jnp.swapaxes(ssm_conv_state, 0, 1), lw['ssm_conv_w'], lw['ssm_conv_b'].reshape(1, xw),
      dt_raw, pad_h(lw['dt_bias']), pad_h(-jnp.exp(lw['a_log'])), jnp.asarray(expand, dtype=BF16))


def _decode_state_kernel(h_ref, xcol_ref, dcol_ref, xdt_ref, dec_ref, xs_ref, z_ref, bm_ref, cm_ref,
                         dsk_ref, ng_ref, hn_ref, o_ref, *, inner):
    half = inner // SSM_GROUPS
    h = h_ref[0]
    row = lax.broadcasted_iota(jnp.int32, h.shape, 0)
    b0 = bm_ref[0][:, 0:SSM_STATE]
    b1 = bm_ref[0][:, SSM_STATE:2 * SSM_STATE]
    c0 = cm_ref[0][:, 0:SSM_STATE]
    c1 = cm_ref[0][:, SSM_STATE:2 * SSM_STATE]
    hn_ref[0] = h * dcol_ref[0] + xcol_ref[0] * jnp.where(row < half, b0, b1)

    def c_dot_h(c_row, h_rows):
        ch, cm_, _ = _split3(jnp.broadcast_to(c_row, (SUBLANES, SSM_STATE)))
        hh, hm, _ = _split3(h_rows)
        return (_dot_nt(ch, hh) + _dot_nt(ch, hm) + _dot_nt(cm_, hh))[0:1, :]

    ch = jnp.concatenate([c_dot_h(c0, h[0:half]), c_dot_h(c1, h[half:inner])], axis=1)
    lane = lax.broadcasted_iota(jnp.int32, (1, inner), 1)
    cb = jnp.where(lane < half, jnp.sum(c0 * b0, axis=-1, keepdims=True), jnp.sum(c1 * b1, axis=-1, keepdims=True))
    y = ch * dec_ref[0] + cb * xdt_ref[0] + dsk_ref[...] * xs_ref[0]
    gated = y * _silu(z_ref[0])
    o_ref[0] = _rmsnorm_rows(gated, ng_ref[...]).astype(o_ref.dtype)


def _decode_state(h0, xdt, dec, xs, z, bm, cm, lw, inner):
    s = h0.shape[0]
    gs = SSM_GROUPS * SSM_STATE
    per = lambda shape: pl.BlockSpec((1,) + shape, lambda b: (b, 0, 0))
    r3 = lambda a: a.reshape(s, 1, a.shape[-1])
    return pl.pallas_call(
        functools.partial(_decode_state_kernel, inner=inner), grid=(s,),
        in_specs=[per((inner, SSM_STATE)), per((inner, 1)), per((inner, 1)), per((1, inner)), per((1, inner)),
                  per((1, inner)), per((1, inner)), per((1, gs)), per((1, gs)), _resident((1, inner)),
                  _resident((1, inner))],
        out_specs=[per((inner, SSM_STATE)), per((1, inner))],
        out_shape=[jax.ShapeDtypeStruct((s, inner, SSM_STATE), F32), jax.ShapeDtypeStruct((s, 1, inner), BF16)],
        compiler_params=_cparams(("parallel",)), name="decode_ssm_state",
    )(h0, xdt.reshape(s, inner, 1), dec.reshape(s, inner, 1), r3(xdt), r3(dec), r3(xs), r3(z), r3(bm), r3(cm),
      jnp.repeat(lw['d_skip'], SSM_HEAD_DIM).reshape(1, inner), lw['ssm_norm_g'].reshape(1, inner))


def _dot_f32_pair(a, b, nt=False):
    f = _dot_nt if nt else (lambda x, y: jnp.dot(x, y, preferred_element_type=F32))
    a_hi, a_mid, _ = _split3(a)
    b_hi, b_mid, _ = _split3(b)
    return f(a_hi, b_hi) + f(a_hi, b_mid) + f(a_mid, b_hi)


def _gate_topk_kernel(gate_ref, o_ref, *, n_blocks):
    gate = jnp.sum(gate_ref[0], axis=0)
    blk = lax.broadcasted_iota(jnp.int32, gate.shape, 1)
    gate = jnp.where(blk < n_blocks, gate, -jnp.inf)
    out = jnp.zeros(gate.shape, jnp.int32)
    for r in range(MOBA_TOPK):
        best = jnp.max(gate, axis=-1, keepdims=True)
        idx = jnp.min(jnp.where(gate == best, blk, n_blocks - 1), axis=-1, keepdims=True)
        out = jnp.where(blk == r, idx, out)
        gate = jnp.where(blk == idx, -jnp.inf, gate)
    o_ref[0] = out


def _gate_topk(gates, n_blocks):
    s, slabs, n_heads, _ = gates.shape
    assert MOBA_TOPK <= n_blocks <= LANES
    out = pl.pallas_call(
        functools.partial(_gate_topk_kernel, n_blocks=n_blocks), grid=(s,),
        in_specs=[pl.BlockSpec((1, slabs, n_heads, LANES), lambda b: (b, 0, 0, 0))],
        out_specs=pl.BlockSpec((1, n_heads, LANES), lambda b: (b, 0, 0)),
        out_shape=jax.ShapeDtypeStruct((s, n_heads, LANES), jnp.int32),
        compiler_params=_cparams(("parallel",)), name="decode_gate_topk",
    )(gates)
    return out[:, :, :MOBA_TOPK]


def _moba_decode_kernel(pt_ref, pick_ref, q_ref, kn_ref, vn_ref, *refs, n_pages_sel):
    del pt_ref, pick_ref
    o_ref = refs[-1]
    kp, vp = refs[:n_pages_sel], refs[n_pages_sel:2 * n_pages_sel]
    q = q_ref[0, 0]
    q_rows = jnp.broadcast_to(q, (SUBLANES, q.shape[1]))
    s_own = jnp.sum(kn_ref[0, 0] * q, axis=-1, keepdims=True)
    ss = [_dot_f32_pair(q_rows, kr[0, 0, 0])[0:1, :] for kr in kp]
    m = s_own
    for s in ss:
        m = jnp.maximum(m, jnp.max(s, axis=-1, keepdims=True))
    p_own = jnp.exp(s_own - m)
    den = p_own
    acc = p_own * vn_ref[0, 0]
    for s, vr in zip(ss, vp):
        p = jnp.exp(s - m)
        den = den + jnp.sum(p, axis=-1, keepdims=True)
        acc = acc + _dot_f32_pair(jnp.broadcast_to(p, (SUBLANES, p.shape[1])), vr[0, 0, 0], nt=True)[0:1, :]
    o_ref[0, 0] = acc / den


def _moba_decode(q, k_new, v_new, cache_kt, cache_vt, page_table, picks, layer):
    s, w = q.shape
    _, _, n_heads, hd, page = cache_kt.shape
    n_pages = page_table.shape[1]
    ppb = MOBA_BLOCK // page
    n_sel = MOBA_TOPK * ppb

    def page_spec(r, t):
        def imap(b, h, pt, pk):
            blk = pk[(b * n_heads + h) * MOBA_TOPK + r]
            return (layer, pt[b * n_pages + blk * ppb + t], h, 0, 0)
        return pl.BlockSpec((1, 1, 1, hd, page), imap)

    vec = pl.BlockSpec((1, 1, 1, hd), lambda b, h, pt, pk: (b, h, 0, 0))
    page_specs = [page_spec(r, t) for r in range(MOBA_TOPK) for t in range(ppb)]
    grid_spec = pltpu.PrefetchScalarGridSpec(
        num_scalar_prefetch=2, grid=(s, n_heads), in_specs=[vec, vec, vec] + page_specs + page_specs,
        out_specs=vec)
    rows = lambda a: a.reshape(s, n_heads, 1, hd)
    out = pl.pallas_call(
        functools.partial(_moba_decode_kernel, n_pages_sel=n_sel), grid_spec=grid_spec,
        out_shape=jax.ShapeDtypeStruct((s, n_heads, 1, hd), F32),
        compiler_params=_cparams(("parallel", "parallel")), name="moba_decode",
    )(page_table.reshape(-1), picks.reshape(-1), rows(q), rows(k_new), rows(v_new),
      *([cache_kt] * n_sel), *([cache_vt] * n_sel))
    return out.reshape(s, w)


PROJ_TILE = 256
CONV_TILE = 256
SSD_TILE = 256
MERGE_TILE = 512
FFN_TILE = 256


def _prep_layer_weights(lw, dims):
    d_model, cw, aw, sw, xw, n_ssm_heads = dims
    w_in = lw['w_in']
    main = 2 * cw + 3 * aw + sw + xw
    n_heads = aw // HEAD_DIM
    hidden = lw['w_ffn_out'].shape[0]
    proj = {
        'g1': lw['norm1_g'].reshape(1, d_model),
        'wm': w_in[:, :main].astype(BF16),
        'wdt': jnp.pad(w_in[:, main:main + n_ssm_heads], ((0, 0), (0, LANES - n_ssm_heads))).astype(BF16),
        'wg': w_in[:, main + n_ssm_heads:].astype(BF16),
        'qg': jnp.tile(lw['q_norm_g'], n_heads).reshape(1, aw),
        'kg': jnp.tile(lw['k_norm_g'], n_heads).reshape(1, aw),
        'pm': _head_mean_matrix(aw),
    }
    merge = {'wc': lw['w_branch_conv'].astype(BF16), 'wa': lw['w_branch_att'].astype(BF16),
             'ws': lw['w_branch_ssm'].astype(BF16), 'wo': lw['w_out'].astype(BF16)}
    ffn = {'g2': lw['norm2_g'].reshape(1, d_model), 'wa': lw['w_ffn_in'][:, :hidden].astype(BF16),
           'wg': lw['w_ffn_in'][:, hidden:].astype(BF16), 'wo': lw['w_ffn_out'].astype(BF16)}
    return proj, merge, ffn


def kernel(x_prompt, x_sample, cache_k, cache_v, state_conv, state_ssm_conv, state_ssm, page_table,
           norm1_g, w_in, conv_w, conv_b, conv_ln_g, conv_ln_b, q_norm_g, k_norm_g, ssm_conv_w, ssm_conv_b,
           dt_bias, a_log, d_skip, ssm_norm_g, w_branch_conv, w_branch_att, w_branch_ssm, w_out, norm2_g,
           w_ffn_in, w_ffn_out):
    bp, lp, d_model = x_prompt.shape
    bs, ls, _ = x_sample.shape
    depth, n_pool, page, att_heads, head_dim = cache_k.shape
    assert head_dim == HEAD_DIM and ls == 1
    cw = conv_w.shape[-1]
    aw = att_heads * HEAD_DIM
    sw = ssm_norm_g.shape[-1]
    xw = ssm_conv_w.shape[-1]
    n_ssm_heads = a_log.shape[-1]
    assert xw == sw + 2 * SSM_GROUPS * SSM_STATE and sw == n_ssm_heads * SSM_HEAD_DIM
    assert conv_w.shape[1] == CONV_WIDTH and ssm_conv_w.shape[1] == SSM_CONV
    dims = (d_model, cw, aw, sw, xw, n_ssm_heads)
    past_len = page_table.shape[1] * page
    assert past_len % MOBA_BLOCK == 0 and lp % PROJ_TILE == 0

    cos_p, sin_p = _rope_tables(jnp.arange(lp, dtype=jnp.int32), att_heads)
    cos_s, sin_s = _rope_tables(jnp.full((bs,), past_len, jnp.int32), att_heads)
    cache_kt = jnp.transpose(cache_k, (0, 1, 3, 4, 2))
    cache_vt = jnp.transpose(cache_v, (0, 1, 3, 4, 2))

    hp = x_prompt.reshape(bp * lp, d_model)
    hs = x_sample.reshape(bs, d_model)
    outs = {name: [] for name in ('kp', 'vp', 'cp', 'scp', 'sp', 'ks', 'vs', 'cs', 'scs', 'ss')}
    for l in range(depth):
        lw = {'w_in': w_in[l], 'norm1_g': norm1_g[l], 'q_norm_g': q_norm_g[l], 'k_norm_g': k_norm_g[l],
              'conv_w': conv_w[l], 'conv_b': conv_b[l], 'conv_ln_g': conv_ln_g[l], 'conv_ln_b': conv_ln_b[l],
              'ssm_conv_w': ssm_conv_w[l], 'ssm_conv_b': ssm_conv_b[l], 'dt_bias': dt_bias[l],
              'a_log': a_log[l], 'd_skip': d_skip[l], 'ssm_norm_g': ssm_norm_g[l],
              'w_branch_conv': w_branch_conv[l], 'w_branch_att': w_branch_att[l],
              'w_branch_ssm': w_branch_ssm[l], 'w_out': w_out[l], 'norm2_g': norm2_g[l],
              'w_ffn_in': w_ffn_in[l], 'w_ffn_out': w_ffn_out[l]}
        pw, mw, fw = _prep_layer_weights(lw, dims)

        sample_proj = _project(hs, pw, cos_s, sin_s, dims, bs, 1, block_stats=False)

        u, q, k, v, z, xbc, dt_raw, gates, k_b, v_b, kmean = _project(
            hp, pw, cos_p, sin_p, dims, PROJ_TILE, lp // PROJ_TILE, block_stats=True)
        seq = lambda a: a.reshape(bp, lp, a.shape[-1])
        o_conv = _conv_branch_seq(seq(u), lw, CONV_TILE)
        o_att = _moba_seq(seq(q), seq(k_b), seq(v_b), kmean.reshape(bp, lp // MOBA_BLOCK, aw))
        o_ssm, h_fin = _ssd_seq(seq(xbc), seq(dt_raw), seq(z), lw, SSD_TILE)
        flat = lambda a: a.reshape(bp * lp, a.shape[-1])
        hp = _merge(hp, flat(o_conv), flat(o_att), flat(o_ssm), gates, mw, MERGE_TILE)
        hp, cache_gates = _ffn(hp, fw, FFN_TILE, paged_keys=(cache_kt, page_table, l, sample_proj[1]))
        by_pos = lambda a: jnp.transpose(a.reshape(bp, att_heads, HEAD_DIM, lp), (0, 3, 1, 2))
        outs['kp'].append(by_pos(k))
        outs['vp'].append(by_pos(v))
        outs['cp'].append(seq(u)[:, lp - (CONV_WIDTH - 1):])
        outs['scp'].append(seq(xbc)[:, lp - (SSM_CONV - 1):])
        outs['sp'].append(h_fin.reshape(bp, n_ssm_heads, SSM_HEAD_DIM, SSM_STATE))

        u, q, k, v, z, xbc, dt_raw, gates = sample_proj
        o_conv, xs, bm, cm, xdt, dec = _decode_mix(u, state_conv[l], xbc, state_ssm_conv[l], dt_raw, lw, sw)
        h_new, o_ssm = _decode_state(state_ssm[l].reshape(bs, sw, SSM_STATE), xdt, dec, xs, z, bm, cm, lw, sw)
        picks = _gate_topk(cache_gates, past_len // MOBA_BLOCK)
        o_att = _moba_decode(q, k, v, cache_kt, cache_vt, page_table, picks, l)
        hs = _merge(hs, o_conv, o_att.astype(BF16), o_ssm.reshape(bs, sw), gates, mw, bs)
        hs = _ffn(hs, fw, bs)
        outs['ks'].append(k.reshape(bs, 1, att_heads, HEAD_DIM))
        outs['vs'].append(v.reshape(bs, 1, att_heads, HEAD_DIM))
        outs['cs'].append(jnp.concatenate([state_conv[l][:, 1:], u[:, None, :]], axis=1))
        outs['scs'].append(jnp.concatenate([state_ssm_conv[l][:, 1:], xbc[:, None, :]], axis=1))
        outs['ss'].append(h_new.reshape(bs, n_ssm_heads, SSM_HEAD_DIM, SSM_STATE))

    st = lambda name: jnp.stack(outs[name])
    return (hp.reshape(bp, lp, d_model), hs.reshape(bs, 1, d_model),
            st('kp'), st('vp'), st('cp'), st('scp'), st('sp'),
            st('ks'), st('vs'), st('cs'), st('scs'), st('ss'))
```
